```python
import jax, jax.numpy as jnp
from jax import lax
import numpy as np

D_MODEL = 2048
BATCH = 1
SEQ = 16384
DEPTH = 1

N_HEADS = 16
QK_NOPE_DIM = 128
QK_ROPE_DIM = 64
V_HEAD_DIM = 128
Q_LORA_RANK = 768
KV_LORA_RANK = 512
CONV_CHANNELS = D_MODEL
CONV_WIDTH = 31
CONV_PAD = CONV_WIDTH // 2
D_FF = 5632
ROPE_THETA = 10000.0
NORM_EPS = 1e-6
Q_BLOCK = 128
POS_OFFSET_MAX = 1024

COLS_CONV = 2 * CONV_CHANNELS
COLS_QDOWN = Q_LORA_RANK
COLS_KVDOWN = KV_LORA_RANK
COLS_KROPE = QK_ROPE_DIM
COLS_GATE_CONV = D_MODEL
COLS_GATE_MLA = D_MODEL
IN_COLS = COLS_CONV + COLS_QDOWN + COLS_KVDOWN + COLS_KROPE + COLS_GATE_CONV + COLS_GATE_MLA
IN_SPLITS = list(np.cumsum([COLS_CONV, COLS_QDOWN, COLS_KVDOWN, COLS_KROPE, COLS_GATE_CONV]).tolist())

kernel_name = "hybrid_conformer_mla_gated_encoder"


def rms_norm(x, g):
    xf = x.astype(jnp.float32)
    y = xf * lax.rsqrt(jnp.mean(xf * xf, axis=-1, keepdims=True) + NORM_EPS)
    return (y * g.astype(jnp.float32)).astype(x.dtype)


def layer_norm(x, g, b):
    xf = x.astype(jnp.float32)
    mu = jnp.mean(xf, axis=-1, keepdims=True)
    xc = xf - mu
    var = jnp.mean(xc * xc, axis=-1, keepdims=True)
    y = xc * lax.rsqrt(var + NORM_EPS)
    return (y * g.astype(jnp.float32) + b.astype(jnp.float32)).astype(x.dtype)


def swiglu_ffn(h, w_gate, w_up, w_down):
    return (jax.nn.silu(h @ w_gate) * (h @ w_up)) @ w_down


def rope_cos_sin(positions):
    inv_freq = ROPE_THETA ** (-jnp.arange(0, QK_ROPE_DIM, 2, dtype=jnp.float32) / QK_ROPE_DIM)
    ang = positions.astype(jnp.float32)[..., None] * inv_freq
    return jnp.cos(ang), jnp.sin(ang)


def apply_rope(x, cos, sin):
    xf = x.astype(jnp.float32)
    x1, x2 = jnp.split(xf, 2, axis=-1)
    return jnp.concatenate([x1 * cos - x2 * sin, x2 * cos + x1 * sin], axis=-1).astype(x.dtype)


def conformer_conv_branch(u, w_dw, b_dw, ln_g, ln_b, w_pw_out):
    a, gate = jnp.split(u, 2, axis=-1)
    h = a * jax.nn.sigmoid(gate)
    h = lax.conv_general_dilated(
        h, w_dw[:, None, :].astype(h.dtype), window_strides=(1,),
        padding=[(CONV_PAD, CONV_PAD)], dimension_numbers=("NWC", "WIO", "NWC"),
        feature_group_count=CONV_CHANNELS) + b_dw
    h = jax.nn.silu(layer_norm(h, ln_g, ln_b))
    return h @ w_pw_out


def mla_branch(c_q_raw, c_kv_raw, k_rope_raw, cos, sin, q_norm_g, w_uq, kv_norm_g, w_ukv, w_o):
    B, S, _ = c_q_raw.shape
    c_q = rms_norm(c_q_raw, q_norm_g)
    q = (c_q @ w_uq).reshape(B, S, N_HEADS, QK_NOPE_DIM + QK_ROPE_DIM)
    q_nope, q_rope = q[..., :QK_NOPE_DIM], q[..., QK_NOPE_DIM:]
    c_kv = rms_norm(c_kv_raw, kv_norm_g)
    kv = (c_kv @ w_ukv).reshape(B, S, N_HEADS, QK_NOPE_DIM + V_HEAD_DIM)
    k_nope, v = kv[..., :QK_NOPE_DIM], kv[..., QK_NOPE_DIM:]
    q_rope = apply_rope(q_rope, cos[:, :, None, :], sin[:, :, None, :])
    k_rope = apply_rope(k_rope_raw, cos, sin)
    scale = (QK_NOPE_DIM + QK_ROPE_DIM) ** -0.5
    n_blk = S // Q_BLOCK
    qn_blk = q_nope.reshape(B, n_blk, Q_BLOCK, N_HEADS, QK_NOPE_DIM).transpose(1, 0, 2, 3, 4)
    qr_blk = q_rope.reshape(B, n_blk, Q_BLOCK, N_HEADS, QK_ROPE_DIM).transpose(1, 0, 2, 3, 4)

    def attend(blk):
        qn, qr = blk
        s = (jnp.einsum('bqhd,bkhd->bhqk', qn, k_nope, preferred_element_type=jnp.float32)
             + jnp.einsum('bqhr,bkr->bhqk', qr, k_rope, preferred_element_type=jnp.float32))
        p = jax.nn.softmax(s * scale, axis=-1)
        return jnp.einsum('bhqk,bkhd->bqhd', p.astype(v.dtype), v)

    o = lax.map(attend, (qn_blk, qr_blk))
    o = o.transpose(1, 0, 2, 3, 4).reshape(B, S, N_HEADS * V_HEAD_DIM)
    return o @ w_o


def setup_inputs(seed: int = 0) -> dict:
    key = jax.random.key(seed)
    ks = jax.random.split(key, 32)
    f32 = jnp.float32
    L = DEPTH

    def nrm(k, shape, fan_in):
        return jax.random.normal(k, shape, f32) * (fan_in ** -0.5)

    def gain(k, shape):
        return 1.0 + 0.02 * jax.random.normal(k, shape, f32)

    x = jax.random.normal(ks[0], (BATCH, SEQ, D_MODEL), f32)
    offs = jax.random.randint(ks[1], (BATCH, 1), 0, POS_OFFSET_MAX, dtype=jnp.int32)
    positions = (jnp.arange(SEQ, dtype=jnp.int32)[None, :] + offs).astype(jnp.int32)
    return {
        "x": x,
        "positions": positions,
        "ffn1_norm_g": gain(ks[2], (L, D_MODEL)),
        "ffn1_w_gate": nrm(ks[3], (L, D_MODEL, D_FF), D_MODEL),
        "ffn1_w_up": nrm(ks[4], (L, D_MODEL, D_FF), D_MODEL),
        "ffn1_w_down": nrm(ks[5], (L, D_FF, D_MODEL), D_FF),
        "mix_norm_g": gain(ks[6], (L, D_MODEL)),
        "w_in": nrm(ks[7], (L, D_MODEL, IN_COLS), D_MODEL),
        "conv_w_dw": nrm(ks[8], (L, CONV_WIDTH, CONV_CHANNELS), CONV_WIDTH),
        "conv_b_dw": 0.02 * jax.random.normal(ks[9], (L, CONV_CHANNELS), f32),
        "conv_ln_g": gain(ks[10], (L, CONV_CHANNELS)),
        "conv_ln_b": 0.02 * jax.random.normal(ks[11], (L, CONV_CHANNELS), f32),
        "conv_w_pw_out": nrm(ks[12], (L, CONV_CHANNELS, D_MODEL), CONV_CHANNELS),
        "mla_q_norm_g": gain(ks[13], (L, Q_LORA_RANK)),
        "mla_w_uq": nrm(ks[14], (L, Q_LORA_RANK, N_HEADS * (QK_NOPE_DIM + QK_ROPE_DIM)), Q_LORA_RANK),
        "mla_kv_norm_g": gain(ks[15], (L, KV_LORA_RANK)),
        "mla_w_ukv": nrm(ks[16], (L, KV_LORA_RANK, N_HEADS * (QK_NOPE_DIM + V_HEAD_DIM)), KV_LORA_RANK),
        "mla_w_o": nrm(ks[17], (L, N_HEADS * V_HEAD_DIM, D_MODEL), N_HEADS * V_HEAD_DIM),
        "w_out": nrm(ks[18], (L, D_MODEL, D_MODEL), D_MODEL),
        "ffn2_norm_g": gain(ks[19], (L, D_MODEL)),
        "ffn2_w_gate": nrm(ks[20], (L, D_MODEL, D_FF), D_MODEL),
        "ffn2_w_up": nrm(ks[21], (L, D_MODEL, D_FF), D_MODEL),
        "ffn2_w_down": nrm(ks[22], (L, D_FF, D_MODEL), D_FF),
        "final_norm_g": gain(ks[23], (D_MODEL,)),
    }


def reference(x, positions, ffn1_norm_g, ffn1_w_gate, ffn1_w_up, ffn1_w_down, mix_norm_g, w_in,
              conv_w_dw, conv_b_dw, conv_ln_g, conv_ln_b, conv_w_pw_out, mla_q_norm_g, mla_w_uq,
              mla_kv_norm_g, mla_w_ukv, mla_w_o, w_out, ffn2_norm_g, ffn2_w_gate, ffn2_w_up,
              ffn2_w_down, final_norm_g):
    cos, sin = rope_cos_sin(positions)
    for l in range(DEPTH):
        x = x + 0.5 * swiglu_ffn(rms_norm(x, ffn1_norm_g[l]), ffn1_w_gate[l], ffn1_w_up[l], ffn1_w_down[l])
        h = rms_norm(x, mix_norm_g[l])
        u = h @ w_in[l]
        u_conv, c_q_raw, c_kv_raw, k_rope_raw, gl_conv, gl_mla = jnp.split(u, IN_SPLITS, axis=-1)
        y_conv = conformer_conv_branch(u_conv, conv_w_dw[l], conv_b_dw[l], conv_ln_g[l], conv_ln_b[l],
                                       conv_w_pw_out[l])
        y_mla = mla_branch(c_q_raw, c_kv_raw, k_rope_raw, cos, sin, mla_q_norm_g[l], mla_w_uq[l],
                           mla_kv_norm_g[l], mla_w_ukv[l], mla_w_o[l])
        merged = jax.nn.sigmoid(gl_conv) * y_conv + jax.nn.sigmoid(gl_mla) * y_mla
        x = x + merged @ w_out[l]
        x = x + 0.5 * swiglu_ffn(rms_norm(x, ffn2_norm_g[l]), ffn2_w_gate[l], ffn2_w_up[l], ffn2_w_down[l])
    return rms_norm(x, final_norm_g)
```

```python
import functools
import math

import jax
import jax.numpy as jnp
from jax import lax
from jax.experimental import pallas as pl
from jax.experimental.pallas import tpu as pltpu

N_HEADS = 16
QK_NOPE_DIM = 128
QK_ROPE_DIM = 64
V_HEAD_DIM = 128
Q_LORA_RANK = 768
KV_LORA_RANK = 512
CONV_WIDTH = 31
CONV_PAD = CONV_WIDTH // 2
ROPE_THETA = 10000.0
NORM_EPS = 1e-6

LANES = 128
SUBLANES = 8
QK_PAD_DIM = 256
HALO_ROWS = 16
VMEM_LIMIT = 56 * 1024 * 1024

F32 = jnp.float32
BF16 = jnp.bfloat16


def _params(*sem):
    return pltpu.CompilerParams(dimension_semantics=sem, vmem_limit_bytes=VMEM_LIMIT)


def _rms(x, g):
    return x * lax.rsqrt(jnp.mean(x * x, axis=-1, keepdims=True) + NORM_EPS) * g


def _ffn_kernel(x_ref, g_ref, wg_ref, wu_ref, wd_ref, g2_ref, *refs, final):
    if final:
        o_ref, hn_scr, acc_scr = refs
    else:
        o_ref, hn_out_ref, hn_scr, acc_scr = refs
    j = pl.program_id(1)

    @pl.when(j == 0)
    def _():
        hn_scr[...] = _rms(x_ref[...], g_ref[...]).astype(BF16)
        acc_scr[...] = jnp.zeros_like(acc_scr)

    h = hn_scr[...]
    a = jnp.dot(h, wg_ref[...], preferred_element_type=F32)
    u = jnp.dot(h, wu_ref[...], preferred_element_type=F32)
    act = (a * jax.nn.sigmoid(a) * u).astype(BF16)
    acc_scr[...] += jnp.dot(act, wd_ref[...], preferred_element_type=F32)

    @pl.when(j == pl.num_programs(1) - 1)
    def _():
        y = x_ref[...] + 0.5 * acc_scr[...]
        if final:
            o_ref[...] = _rms(y, g2_ref[...])
        else:
            o_ref[...] = y
            hn_out_ref[...] = _rms(y, g2_ref[...]).astype(BF16)


def _ffn(x, g, wg, wu, wd, g2, *, final, tm=512, tf=512):
    S, D = x.shape
    F = wg.shape[1]
    tm = min(tm, S)
    row = pl.BlockSpec((tm, D), lambda i, j: (i, 0))
    vec = pl.BlockSpec((1, D), lambda i, j: (0, 0))
    out_shape = [jax.ShapeDtypeStruct((S, D), F32)]
    out_specs = [row]
    if not final:
        out_shape.append(jax.ShapeDtypeStruct((S, D), BF16))
        out_specs.append(row)
    return pl.pallas_call(
        functools.partial(_ffn_kernel, final=final),
        grid=(S // tm, F // tf),
        in_specs=[row, vec,
                  pl.BlockSpec((D, tf), lambda i, j: (0, j)),
                  pl.BlockSpec((D, tf), lambda i, j: (0, j)),
                  pl.BlockSpec((tf, D), lambda i, j: (j, 0)),
                  vec],
        out_specs=out_specs,
        out_shape=out_shape,
        scratch_shapes=[pltpu.VMEM((tm, D), BF16), pltpu.VMEM((tm, D), F32)],
        compiler_params=_params("parallel", "arbitrary"),
        name="ffn_final" if final else "ffn_mix",
    )(x, g, wg, wu, wd, g2)


def _glu_kernel(h_ref, wa_ref, wg_ref, o_ref):
    h = h_ref[...]
    a = jnp.dot(h, wa_ref[...], preferred_element_type=F32)
    g = jnp.dot(h, wg_ref[...], preferred_element_type=F32)
    o_ref[...] = a * jax.nn.sigmoid(g)


def _glu_proj(hn, w_glu, *, tm=1024, tn=512):
    S, D = hn.shape
    C = w_glu.shape[1] // 2
    tm = min(tm, S)
    nj = C // tn
    return pl.pallas_call(
        _glu_kernel,
        grid=(S // tm, nj),
        in_specs=[pl.BlockSpec((tm, D), lambda i, j: (i, 0)),
                  pl.BlockSpec((D, tn), lambda i, j: (0, j)),
                  pl.BlockSpec((D, tn), lambda i, j: (0, j + nj))],
        out_specs=pl.BlockSpec((tm, tn), lambda i, j: (i, j)),
        out_shape=jax.ShapeDtypeStruct((S, C), F32),
        compiler_params=_params("parallel", "arbitrary"),
        name="glu_proj",
    )(hn, w_glu, w_glu)


def _small_kernel(h_ref, w_ref, gq_ref, gkv_ref, pos_ref, freq_ref,
                  cq_ref, ckv_ref, kr_ref, cs_ref, sn_ref):
    r = jnp.dot(h_ref[...], w_ref[...], preferred_element_type=F32)
    cq_ref[...] = _rms(r[:, :Q_LORA_RANK], gq_ref[...]).astype(BF16)
    c0 = Q_LORA_RANK
    ckv_ref[...] = _rms(r[:, c0:c0 + KV_LORA_RANK], gkv_ref[...]).astype(BF16)
    c1 = c0 + KV_LORA_RANK
    ang = pos_ref[...].astype(F32) * freq_ref[...]
    cs = jnp.cos(ang)
    sn = jnp.sin(ang)
    cs_ref[...] = cs
    sn_ref[...] = sn
    kr_ref[...] = (r[:, c1:c1 + LANES] * cs + r[:, c1 + LANES:c1 + 2 * LANES] * sn).astype(BF16)


def _small_proj(hn, w_small, gq, gkv, pos, freq, *, tm=512):
    S, D = hn.shape
    N = w_small.shape[1]
    tm = min(tm, S)
    row = lambda n: pl.BlockSpec((tm, n), lambda i: (i, 0))
    full = lambda a: pl.BlockSpec(a.shape, lambda i: (0, 0))
    return pl.pallas_call(
        _small_kernel,
        grid=(S // tm,),
        in_specs=[row(D), full(w_small), full(gq), full(gkv), row(1), full(freq)],
        out_specs=[row(Q_LORA_RANK), row(KV_LORA_RANK), row(LANES), row(LANES), row(LANES)],
        out_shape=[jax.ShapeDtypeStruct((S, Q_LORA_RANK), BF16),
                   jax.ShapeDtypeStruct((S, KV_LORA_RANK), BF16),
                   jax.ShapeDtypeStruct((S, LANES), BF16),
                   jax.ShapeDtypeStruct((S, LANES), F32),
                   jax.ShapeDtypeStruct((S, LANES), F32)],
        compiler_params=_params("parallel"),
        name="small_proj",
    )(hn, w_small, gq, gkv, pos, freq)


def _qkv_kernel(cq_ref, ckv_ref, kr_ref, cs_ref, sn_ref, wq_ref, wkv_ref,
                q_ref, k_ref, v_ref, *, q_scale):
    cq = cq_ref[...]
    ckv = ckv_ref[...]
    kr = kr_ref[...]
    cs = cs_ref[...]
    sn = sn_ref[...]
    for h in range(N_HEADS):
        r = jnp.dot(cq, wq_ref[h], preferred_element_type=F32)
        q_ref[h, :, 0:LANES] = (r[:, 0:LANES] * q_scale).astype(BF16)
        q_rope = r[:, LANES:2 * LANES] * cs + r[:, 2 * LANES:3 * LANES] * sn
        q_ref[h, :, LANES:2 * LANES] = (q_rope * q_scale).astype(BF16)
        kv = jnp.dot(ckv, wkv_ref[h], preferred_element_type=F32)
        k_ref[h, :, 0:LANES] = kv[:, 0:LANES].astype(BF16)
        k_ref[h, :, LANES:2 * LANES] = kr
        v_ref[h] = kv[:, LANES:2 * LANES].astype(BF16)


def _qkv_proj(cq, ckv, kr, cs, sn, wq, wkv, *, q_scale, tm=256):
    S = cq.shape[0]
    tm = min(tm, S)
    row = lambda n: pl.BlockSpec((tm, n), lambda i: (i, 0))
    full3 = lambda a: pl.BlockSpec(a.shape, lambda i: (0, 0, 0))
    head = lambda n: pl.BlockSpec((N_HEADS, tm, n), lambda i: (0, i, 0))
    return pl.pallas_call(
        functools.partial(_qkv_kernel, q_scale=q_scale),
        grid=(S // tm,),
        in_specs=[row(Q_LORA_RANK), row(KV_LORA_RANK), row(LANES), row(LANES), row(LANES),
                  full3(wq), full3(wkv)],
        out_specs=[head(QK_PAD_DIM), head(QK_PAD_DIM), head(V_HEAD_DIM)],
        out_shape=[jax.ShapeDtypeStruct((N_HEADS, S, QK_PAD_DIM), BF16),
                   jax.ShapeDtypeStruct((N_HEADS, S, QK_PAD_DIM), BF16),
                   jax.ShapeDtypeStruct((N_HEADS, S, V_HEAD_DIM), BF16)],
        compiler_params=_params("parallel"),
        name="qkv_proj",
    )(cq, ckv, kr, cs, sn, wq, wkv)


def _attn_kernel(q_ref, k_ref, v_ref, o_ref, *, tk):
    q = q_ref[...]
    tq = q.shape[0]
    n_kv = k_ref.shape[0] // tk

    def body(t, carry):
        m, l, acc = carry
        start = pl.multiple_of(t * tk, tk)
        k = k_ref[pl.ds(start, tk), :]
        s = lax.dot_general(q, k, (((1,), (1,)), ((), ())), preferred_element_type=F32)
        m_new = jnp.maximum(m, jnp.max(s, axis=1, keepdims=True))
        alpha = jnp.exp2(m - m_new)
        p = jnp.exp2(s - m_new)
        l = alpha * l + jnp.sum(p, axis=1, keepdims=True)
        pv = jnp.dot(p.astype(BF16), v_ref[pl.ds(start, tk), :], preferred_element_type=F32)
        return m_new, l, alpha * acc + pv

    m0 = jnp.full((tq, 1), -jnp.inf, F32)
    l0 = jnp.zeros((tq, 1), F32)
    acc0 = jnp.zeros((tq, V_HEAD_DIM), F32)
    _, l, acc = lax.fori_loop(0, n_kv, body, (m0, l0, acc0))
    o_ref[...] = (acc / l).astype(BF16)


def _attention(q, k, v, *, tq=512, tk=512):
    H, S, _ = q.shape
    tq = min(tq, S)
    tk = min(tk, S)
    return pl.pallas_call(
        functools.partial(_attn_kernel, tk=tk),
        grid=(H, S // tq),
        in_specs=[pl.BlockSpec((None, tq, QK_PAD_DIM), lambda h, i: (h, i, 0)),
                  pl.BlockSpec((None, S, QK_PAD_DIM), lambda h, i: (h, 0, 0)),
                  pl.BlockSpec((None, S, V_HEAD_DIM), lambda h, i: (h, 0, 0))],
        out_specs=pl.BlockSpec((tq, V_HEAD_DIM), lambda h, i: (i, h)),
        out_shape=jax.ShapeDtypeStruct((S, H * V_HEAD_DIM), BF16),
        compiler_params=_params("parallel", "arbitrary"),
        name="attention",
    )(q, k, v)


def _conv_kernel(prev_ref, cur_ref, next_ref, w_ref, b_ref, g_ref, beta_ref, o_ref,
                 buf, shifted, conv_scr, *, rb, lc):
    i = pl.program_id(0)
    ts, C = cur_ref.shape
    first = i == 0
    last = i == pl.num_programs(0) - 1
    buf[0:HALO_ROWS, :] = jnp.where(first, 0.0, prev_ref[...])
    buf[HALO_ROWS:HALO_ROWS + ts, :] = cur_ref[...]
    buf[HALO_ROWS + ts:, :] = jnp.where(last, 0.0, next_ref[...])
    off = HALO_ROWS - CONV_PAD
    n_sh = ts + SUBLANES * ((CONV_WIDTH + off - 1) // SUBLANES)

    for c0 in range(0, C, lc):
        for b in range(1, SUBLANES):
            shifted[b - 1, 0:n_sh, :] = buf[b:b + n_sh, c0:c0 + lc]

        def rows(rblk, _, c0=c0):
            r0 = pl.multiple_of(rblk * rb, rb)
            acc = jnp.broadcast_to(b_ref[:, c0:c0 + lc], (rb, lc))
            for k in range(CONV_WIDTH):
                a, b = divmod(k + off, SUBLANES)
                start = pl.multiple_of(r0 + a * SUBLANES, SUBLANES)
                if b == 0:
                    tap = buf[pl.ds(start, rb), c0:c0 + lc]
                else:
                    tap = shifted[b - 1, pl.ds(start, rb), :]
                acc = acc + w_ref[k:k + 1, c0:c0 + lc] * tap
            conv_scr[pl.ds(r0, rb), c0:c0 + lc] = acc
            return 0
        lax.fori_loop(0, ts // rb, rows, 0)

    y = conv_scr[...]
    mu = jnp.mean(y, axis=-1, keepdims=True)
    yc = y - mu
    var = jnp.mean(yc * yc, axis=-1, keepdims=True)
    z = yc * lax.rsqrt(var + NORM_EPS) * g_ref[...] + beta_ref[...]
    o_ref[...] = (z * jax.nn.sigmoid(z)).astype(BF16)


def _conv(hglu, w, b, g, beta, *, ts=256, rb=32, lc=512):
    S, C = hglu.shape
    ts = min(ts, S)
    n = S // ts
    hb = ts // HALO_ROWS
    last_halo = S // HALO_ROWS - 1
    vec = pl.BlockSpec((1, C), lambda i: (0, 0))
    return pl.pallas_call(
        functools.partial(_conv_kernel, rb=rb, lc=lc),
        grid=(n,),
        in_specs=[pl.BlockSpec((HALO_ROWS, C), lambda i: (jnp.maximum(i * hb - 1, 0), 0)),
                  pl.BlockSpec((ts, C), lambda i: (i, 0)),
                  pl.BlockSpec((HALO_ROWS, C), lambda i: (jnp.minimum((i + 1) * hb, last_halo), 0)),
                  pl.BlockSpec((CONV_WIDTH, C), lambda i: (0, 0)),
                  vec, vec, vec],
        out_specs=pl.BlockSpec((ts, C), lambda i: (i, 0)),
        out_shape=jax.ShapeDtypeStruct((S, C), BF16),
        scratch_shapes=[pltpu.VMEM((ts + 2 * HALO_ROWS, C), F32),
                        pltpu.VMEM((SUBLANES - 1, ts + 2 * HALO_ROWS, lc), F32),
                        pltpu.VMEM((ts, C), F32)],
        compiler_params=_params("parallel"),
        name="conv",
    )(hglu, hglu, hglu, w, b, g, beta)


def _merge_kernel(hn_ref, hc_ref, o_ref, wgc_ref, wgm_ref, wpw_ref, wo_ref, out_ref):
    hn = hn_ref[...]
    gc = jax.nn.sigmoid(jnp.dot(hn, wgc_ref[...], preferred_element_type=F32))
    gm = jax.nn.sigmoid(jnp.dot(hn, wgm_ref[...], preferred_element_type=F32))
    yc = jnp.dot(hc_ref[...], wpw_ref[...], preferred_element_type=F32)
    ym = jnp.dot(o_ref[...], wo_ref[...], preferred_element_type=F32)
    out_ref[...] = (gc * yc + gm * ym).astype(BF16)


def _merge(hn, hc, o, wgc, wgm, wpw, wo, *, tm=512, tn=512):
    S, D = hn.shape
    tm = min(tm, S)
    row = pl.BlockSpec((tm, D), lambda i, j: (i, 0))
    col = pl.BlockSpec((D, tn), lambda i, j: (0, j))
    return pl.pallas_call(
        _merge_kernel,
        grid=(S // tm, D // tn),
        in_specs=[row, row, row, col, col, col, col],
        out_specs=pl.BlockSpec((tm, tn), lambda i, j: (i, j)),
        out_shape=jax.ShapeDtypeStruct((S, D), BF16),
        compiler_params=_params("parallel", "arbitrary"),
        name="merge",
    )(hn, hc, o, wgc, wgm, wpw, wo)


def _outproj_kernel(x_ref, m_ref, w_ref, o_ref):
    o_ref[...] = x_ref[...] + jnp.dot(m_ref[...], w_ref[...], preferred_element_type=F32)


def _out_proj(x, merged, w, *, tm=512):
    S, D = x.shape
    tm = min(tm, S)
    row = pl.BlockSpec((tm, D), lambda i: (i, 0))
    return pl.pallas_call(
        _outproj_kernel,
        grid=(S // tm,),
        in_specs=[row, row, pl.BlockSpec((D, D), lambda i: (0, 0))],
        out_specs=row,
        out_shape=jax.ShapeDtypeStruct((S, D), F32),
        compiler_params=_params("parallel"),
        name="out_proj",
    )(x, merged, w)


def _rot_half_cols(w):
    half = QK_ROPE_DIM // 2
    return jnp.concatenate([-w[..., half:], w[..., :half]], axis=-1)


def _pad_cols(w, n):
    return jnp.pad(w, [(0, 0)] * (w.ndim - 1) + [(0, n - w.shape[-1])])


def _layer(x, pos, freq, ffn1_norm_g, ffn1_w_gate, ffn1_w_up, ffn1_w_down, mix_norm_g, w_in,
           conv_w_dw, conv_b_dw, conv_ln_g, conv_ln_b, conv_w_pw_out, mla_q_norm_g, mla_w_uq,
           mla_kv_norm_g, mla_w_ukv, mla_w_o, w_out, ffn2_norm_g, ffn2_w_gate, ffn2_w_up,
           ffn2_w_down, out_norm_g, *, final):
    D = x.shape[1]
    C = conv_w_dw.shape[1]
    row = lambda v: v.reshape(1, -1)
    bf = lambda w: w.astype(BF16)

    o_q = 2 * C
    o_kv = o_q + Q_LORA_RANK
    o_kr = o_kv + KV_LORA_RANK
    o_gc = o_kr + QK_ROPE_DIM
    o_gm = o_gc + D
    w_glu = bf(w_in[:, :o_q])
    w_kr = w_in[:, o_kr:o_gc]
    w_small = bf(jnp.concatenate(
        [w_in[:, o_q:o_kr], _pad_cols(w_kr, LANES), _pad_cols(_rot_half_cols(w_kr), LANES)], axis=1))
    w_gc = bf(w_in[:, o_gc:o_gm])
    w_gm = bf(w_in[:, o_gm:])

    wq = mla_w_uq.reshape(Q_LORA_RANK, N_HEADS, QK_NOPE_DIM + QK_ROPE_DIM)
    wq_rope = wq[..., QK_NOPE_DIM:]
    wq = jnp.concatenate([wq[..., :QK_NOPE_DIM], _pad_cols(wq_rope, LANES),
                          _pad_cols(_rot_half_cols(wq_rope), LANES)], axis=-1)
    wq = bf(wq.transpose(1, 0, 2))
    wkv = bf(mla_w_ukv.reshape(KV_LORA_RANK, N_HEADS, QK_NOPE_DIM + V_HEAD_DIM).transpose(1, 0, 2))

    x1, hn = _ffn(x, row(ffn1_norm_g), bf(ffn1_w_gate), bf(ffn1_w_up), bf(ffn1_w_down),
                  row(mix_norm_g), final=False)

    hglu = _glu_proj(hn, w_glu)
    hc = _conv(hglu, conv_w_dw, row(conv_b_dw), row(conv_ln_g), row(conv_ln_b))

    cq, ckv, kr, cs, sn = _small_proj(hn, w_small, row(mla_q_norm_g), row(mla_kv_norm_g), pos, freq)
    q_scale = (QK_NOPE_DIM + QK_ROPE_DIM) ** -0.5 * math.log2(math.e)
    q, k, v = _qkv_proj(cq, ckv, kr, cs, sn, wq, wkv, q_scale=q_scale)
    o = _attention(q, k, v)

    merged = _merge(hn, hc, o, w_gc, w_gm, bf(conv_w_pw_out), bf(mla_w_o))
    x2 = _out_proj(x1, merged, bf(w_out))
    (y,) = _ffn(x2, row(ffn2_norm_g), bf(ffn2_w_gate), bf(ffn2_w_up), bf(ffn2_w_down),
                row(out_norm_g), final=final)
    return y


def kernel(x, positions, ffn1_norm_g, ffn1_w_gate, ffn1_w_up, ffn1_w_down, mix_norm_g, w_in, conv_w_dw, conv_b_dw, conv_ln_g, conv_ln_b, conv_w_pw_out, mla_q_norm_g, mla_w_uq, mla_kv_norm_g, mla_w_ukv, mla_w_o, w_out, ffn2_norm_g, ffn2_w_gate, ffn2_w_up, ffn2_w_down, final_norm_g):
    B, S, D = x.shape
    depth = ffn1_norm_g.shape[0]
    assert depth == 1, "the fused final norm assumes a single layer"
    inv_freq = ROPE_THETA ** (-jnp.arange(0, QK_ROPE_DIM, 2, dtype=F32) / QK_ROPE_DIM)
    freq = _pad_cols(jnp.concatenate([inv_freq, inv_freq]), LANES).reshape(1, LANES)
    outs = []
    for b in range(B):
        y = _layer(x[b], positions[b].reshape(S, 1), freq,
                   ffn1_norm_g[0], ffn1_w_gate[0], ffn1_w_up[0], ffn1_w_down[0], mix_norm_g[0],
                   w_in[0], conv_w_dw[0], conv_b_dw[0], conv_ln_g[0], conv_ln_b[0],
                   conv_w_pw_out[0], mla_q_norm_g[0], mla_w_uq[0], mla_kv_norm_g[0], mla_w_ukv[0],
                   mla_w_o[0], w_out[0], ffn2_norm_g[0], ffn2_w_gate[0], ffn2_w_up[0],
                   ffn2_w_down[0], final_norm_g, final=True)
        outs.append(y)
    return jnp.stack(outs)
```

```python
import functools
import math

import jax
import jax.numpy as jnp
from jax import lax
from jax.experimental import pallas as pl
from jax.experimental.pallas import tpu as pltpu

N_HEADS = 16
QK_NOPE_DIM = 128
QK_ROPE_DIM = 64
V_HEAD_DIM = 128
Q_LORA_RANK = 768
KV_LORA_RANK = 512
CONV_WIDTH = 31
CONV_PAD = CONV_WIDTH // 2
ROPE_THETA = 10000.0
NORM_EPS = 1e-6

LANES = 128
SUBLANES = 8
QK_PAD_DIM = 256
HALO_ROWS = 16
VMEM_LIMIT = 56 * 1024 * 1024

F32 = jnp.float32
BF16 = jnp.bfloat16


def _params(*sem):
    return pltpu.CompilerParams(dimension_semantics=sem, vmem_limit_bytes=VMEM_LIMIT)


def _rms(x, g):
    return x * lax.rsqrt(jnp.mean(x * x, axis=-1, keepdims=True) + NORM_EPS) * g


def _ffn_kernel(x_ref, g_ref, wg_ref, wu_ref, wd_ref, g2_ref, *refs, final):
    if final:
        o_ref, hn_scr, acc_scr = refs
    else:
        o_ref, hn_out_ref, hn_scr, acc_scr = refs
    j = pl.program_id(1)

    @pl.when(j == 0)
    def _():
        hn_scr[...] = _rms(x_ref[...], g_ref[...]).astype(BF16)
        acc_scr[...] = jnp.zeros_like(acc_scr)

    h = hn_scr[...]
    a = jnp.dot(h, wg_ref[...], preferred_element_type=F32)
    u = jnp.dot(h, wu_ref[...], preferred_element_type=F32)
    act = (a * jax.nn.sigmoid(a) * u).astype(BF16)
    acc_scr[...] += jnp.dot(act, wd_ref[...], preferred_element_type=F32)

    @pl.when(j == pl.num_programs(1) - 1)
    def _():
        y = x_ref[...] + 0.5 * acc_scr[...]
        if final:
            o_ref[...] = _rms(y, g2_ref[...])
        else:
            o_ref[...] = y
            hn_out_ref[...] = _rms(y, g2_ref[...]).astype(BF16)


def _ffn(x, g, wg, wu, wd, g2, *, final, tm=512, tf=512):
    S, D = x.shape
    F = wg.shape[1]
    tm = min(tm, S)
    row = pl.BlockSpec((tm, D), lambda i, j: (i, 0))
    vec = pl.BlockSpec((1, D), lambda i, j: (0, 0))
    out_shape = [jax.ShapeDtypeStruct((S, D), F32)]
    out_specs = [row]
    if not final:
        out_shape.append(jax.ShapeDtypeStruct((S, D), BF16))
        out_specs.append(row)
    return pl.pallas_call(
        functools.partial(_ffn_kernel, final=final),
        grid=(S // tm, F // tf),
        in_specs=[row, vec,
                  pl.BlockSpec((D, tf), lambda i, j: (0, j)),
                  pl.BlockSpec((D, tf), lambda i, j: (0, j)),
                  pl.BlockSpec((tf, D), lambda i, j: (j, 0)),
                  vec],
        out_specs=out_specs,
        out_shape=out_shape,
        scratch_shapes=[pltpu.VMEM((tm, D), BF16), pltpu.VMEM((tm, D), F32)],
        compiler_params=_params("parallel", "arbitrary"),
        name="ffn_final" if final else "ffn_mix",
    )(x, g, wg, wu, wd, g2)


def _glu_kernel(h_ref, wa_ref, wg_ref, o_ref):
    h = h_ref[...]
    a = jnp.dot(h, wa_ref[...], preferred_element_type=F32)
    g = jnp.dot(h, wg_ref[...], preferred_element_type=F32)
    o_ref[...] = a * jax.nn.sigmoid(g)


def _glu_proj(hn, w_glu, *, tm=1024, tn=512):
    S, D = hn.shape
    C = w_glu.shape[1] // 2
    tm = min(tm, S)
    nj = C // tn
    return pl.pallas_call(
        _glu_kernel,
        grid=(S // tm, nj),
        in_specs=[pl.BlockSpec((tm, D), lambda i, j: (i, 0)),
                  pl.BlockSpec((D, tn), lambda i, j: (0, j)),
                  pl.BlockSpec((D, tn), lambda i, j: (0, j + nj))],
        out_specs=pl.BlockSpec((tm, tn), lambda i, j: (i, j)),
        out_shape=jax.ShapeDtypeStruct((S, C), F32),
        compiler_params=_params("parallel", "arbitrary"),
        name="glu_proj",
    )(hn, w_glu, w_glu)


def _small_kernel(h_ref, w_ref, gq_ref, gkv_ref, pos_col_ref, freq_row_ref, pos_row_ref, freq_col_ref,
                  cq_ref, ckv_ref, kr_ref, cst_ref, snt_ref):
    r = jnp.dot(h_ref[...], w_ref[...], preferred_element_type=F32)
    cq_ref[...] = _rms(r[:, :Q_LORA_RANK], gq_ref[...]).astype(BF16)
    c0 = Q_LORA_RANK
    ckv_ref[...] = _rms(r[:, c0:c0 + KV_LORA_RANK], gkv_ref[...]).astype(BF16)
    c1 = c0 + KV_LORA_RANK
    ang = pos_col_ref[...].astype(F32) * freq_row_ref[...]
    kr = r[:, c1:c1 + LANES] * jnp.cos(ang) + r[:, c1 + LANES:c1 + 2 * LANES] * jnp.sin(ang)
    kr_ref[...] = kr.astype(BF16)
    ang_t = freq_col_ref[...] * pos_row_ref[...].astype(F32)
    cst_ref[...] = jnp.cos(ang_t)
    snt_ref[...] = jnp.sin(ang_t)


def _small_proj(hn, w_small, gq, gkv, pos, freq, *, tm=512):
    S, D = hn.shape
    tm = min(tm, S)
    row = lambda n: pl.BlockSpec((tm, n), lambda i: (i, 0))
    col = lambda n: pl.BlockSpec((n, tm), lambda i: (0, i))
    full = lambda a: pl.BlockSpec(a.shape, lambda i: (0, 0))
    pos_col, pos_row = pos.reshape(S, 1), pos.reshape(1, S)
    freq_row, freq_col = freq.reshape(1, LANES), freq.reshape(LANES, 1)
    return pl.pallas_call(
        _small_kernel,
        grid=(S // tm,),
        in_specs=[row(D), full(w_small), full(gq), full(gkv), row(1), full(freq_row), col(1), full(freq_col)],
        out_specs=[row(Q_LORA_RANK), row(KV_LORA_RANK), row(LANES), col(LANES), col(LANES)],
        out_shape=[jax.ShapeDtypeStruct((S, Q_LORA_RANK), BF16),
                   jax.ShapeDtypeStruct((S, KV_LORA_RANK), BF16),
                   jax.ShapeDtypeStruct((S, LANES), BF16),
                   jax.ShapeDtypeStruct((LANES, S), F32),
                   jax.ShapeDtypeStruct((LANES, S), F32)],
        compiler_params=_params("parallel"),
        name="small_proj",
    )(hn, w_small, gq, gkv, pos_col, freq_row, pos_row, freq_col)


_NT = (((1,), (1,)), ((), ()))


def _qkv_kernel(cq_ref, ckv_ref, kr_ref, cst_ref, snt_ref, wqt_ref, wk_ref, wvt_ref,
                qt_ref, k_ref, vt_ref, *, q_scale):
    cq = cq_ref[...]
    ckv = ckv_ref[...]
    kr = kr_ref[...]
    cst = cst_ref[...]
    snt = snt_ref[...]
    for h in range(N_HEADS):
        rt = lax.dot_general(wqt_ref[h], cq, _NT, preferred_element_type=F32)
        qt_ref[h, 0:LANES, :] = (rt[0:LANES] * q_scale).astype(BF16)
        q_rope = rt[LANES:2 * LANES] * cst + rt[2 * LANES:3 * LANES] * snt
        qt_ref[h, LANES:2 * LANES, :] = (q_rope * q_scale).astype(BF16)
        k_nope = jnp.dot(ckv, wk_ref[h], preferred_element_type=F32)
        k_ref[h, :, 0:LANES] = k_nope.astype(BF16)
        k_ref[h, :, LANES:2 * LANES] = kr
        vt = lax.dot_general(wvt_ref[h], ckv, _NT, preferred_element_type=F32)
        vt_ref[h] = vt.astype(BF16)


def _qkv_proj(cq, ckv, kr, cst, snt, wqt, wk, wvt, *, q_scale, tm=256):
    S = cq.shape[0]
    tm = min(tm, S)
    row = lambda n: pl.BlockSpec((tm, n), lambda i: (i, 0))
    col = lambda n: pl.BlockSpec((n, tm), lambda i: (0, i))
    full3 = lambda a: pl.BlockSpec(a.shape, lambda i: (0, 0, 0))
    return pl.pallas_call(
        functools.partial(_qkv_kernel, q_scale=q_scale),
        grid=(S // tm,),
        in_specs=[row(Q_LORA_RANK), row(KV_LORA_RANK), row(LANES), col(LANES), col(LANES),
                  full3(wqt), full3(wk), full3(wvt)],
        out_specs=[pl.BlockSpec((N_HEADS, QK_PAD_DIM, tm), lambda i: (0, 0, i)),
                   pl.BlockSpec((N_HEADS, tm, QK_PAD_DIM), lambda i: (0, i, 0)),
                   pl.BlockSpec((N_HEADS, V_HEAD_DIM, tm), lambda i: (0, 0, i))],
        out_shape=[jax.ShapeDtypeStruct((N_HEADS, QK_PAD_DIM, S), BF16),
                   jax.ShapeDtypeStruct((N_HEADS, S, QK_PAD_DIM), BF16),
                   jax.ShapeDtypeStruct((N_HEADS, V_HEAD_DIM, S), BF16)],
        compiler_params=_params("parallel"),
        name="qkv_proj",
    )(cq, ckv, kr, cst, snt, wqt, wk, wvt)


def _attn_kernel(qt_ref, k_ref, vt_ref, o_ref, s_scr, *, tk, unroll):
    qt = qt_ref[...]
    tq = qt.shape[1]
    n_kv = k_ref.shape[0] // tk

    def scores(t):
        start = pl.multiple_of(t * tk, tk)
        return jnp.dot(k_ref[pl.ds(start, tk), :], qt, preferred_element_type=F32)

    def softmax_pv(s, t, carry):
        m, l, acc = carry
        m_new = jnp.maximum(m, jnp.max(s, axis=0, keepdims=True))
        alpha = jnp.exp2(m - m_new)
        p = jnp.exp2(s - m_new)
        l = alpha * l + jnp.sum(p, axis=0, keepdims=True)
        start = pl.multiple_of(t * tk, tk)
        pv = jnp.dot(vt_ref[:, pl.ds(start, tk)], p.astype(BF16), preferred_element_type=F32)
        return m_new, l, alpha * acc + pv

    def body(u, carry):
        t = unroll * u
        for j in range(unroll):
            nxt = t + j + 1
            if j == unroll - 1:
                nxt = jnp.minimum(nxt, n_kv - 1)
            s_scr[(j + 1) % 2] = scores(nxt)
            carry = softmax_pv(s_scr[j % 2], t + j, carry)
        return carry

    s_scr[0] = scores(0)
    m0 = jnp.full((1, tq), -jnp.inf, F32)
    l0 = jnp.zeros((1, tq), F32)
    acc0 = jnp.zeros((V_HEAD_DIM, tq), F32)
    _, l, acc = lax.fori_loop(0, n_kv // unroll, body, (m0, l0, acc0))
    o_ref[...] = (acc / l).T.astype(BF16)


def _attention(qt, k, vt, *, tq=512, tk=512, unroll=4):
    H, S, _ = k.shape
    tq = min(tq, S)
    tk = min(tk, S // unroll)
    assert unroll % 2 == 0 and S % (unroll * tk) == 0 and S % tq == 0
    return pl.pallas_call(
        functools.partial(_attn_kernel, tk=tk, unroll=unroll),
        grid=(H, S // tq),
        in_specs=[pl.BlockSpec((None, QK_PAD_DIM, tq), lambda h, i: (h, 0, i)),
                  pl.BlockSpec((None, S, QK_PAD_DIM), lambda h, i: (h, 0, 0)),
                  pl.BlockSpec((None, V_HEAD_DIM, S), lambda h, i: (h, 0, 0))],
        out_specs=pl.BlockSpec((tq, V_HEAD_DIM), lambda h, i: (i, h)),
        out_shape=jax.ShapeDtypeStruct((S, H * V_HEAD_DIM), BF16),
        scratch_shapes=[pltpu.VMEM((2, tk, tq), F32)],
        compiler_params=_params("parallel", "arbitrary"),
        name="attention",
    )(qt, k, vt)


def _conv_kernel(prev_ref, cur_ref, next_ref, w_ref, b_ref, g_ref, beta_ref, o_ref,
                 buf, shifted, conv_scr, *, rb, lc):
    i = pl.program_id(0)
    ts, C = cur_ref.shape
    first = i == 0
    last = i == pl.num_programs(0) - 1
    buf[0:HALO_ROWS, :] = jnp.where(first, 0.0, prev_ref[...])
    buf[HALO_ROWS:HALO_ROWS + ts, :] = cur_ref[...]
    buf[HALO_ROWS + ts:, :] = jnp.where(last, 0.0, next_ref[...])
    off = HALO_ROWS - CONV_PAD
    n_sh = ts + SUBLANES * ((CONV_WIDTH + off - 1) // SUBLANES)

    for c0 in range(0, C, lc):
        for b in range(1, SUBLANES):
            shifted[b - 1, 0:n_sh, :] = buf[b:b + n_sh, c0:c0 + lc]

        def rows(rblk, _, c0=c0):
            r0 = pl.multiple_of(rblk * rb, rb)
            acc = jnp.broadcast_to(b_ref[:, c0:c0 + lc], (rb, lc))
            for k in range(CONV_WIDTH):
                a, b = divmod(k + off, SUBLANES)
                start = pl.multiple_of(r0 + a * SUBLANES, SUBLANES)
                if b == 0:
                    tap = buf[pl.ds(start, rb), c0:c0 + lc]
                else:
                    tap = shifted[b - 1, pl.ds(start, rb), :]
                acc = acc + w_ref[k:k + 1, c0:c0 + lc] * tap
            conv_scr[pl.ds(r0, rb), c0:c0 + lc] = acc
            return 0
        lax.fori_loop(0, ts // rb, rows, 0)

    y = conv_scr[...]
    mu = jnp.mean(y, axis=-1, keepdims=True)
    yc = y - mu
    var = jnp.mean(yc * yc, axis=-1, keepdims=True)
    z = yc * lax.rsqrt(var + NORM_EPS) * g_ref[...] + beta_ref[...]
    o_ref[...] = (z * jax.nn.sigmoid(z)).astype(BF16)


def _conv(hglu, w, b, g, beta, *, ts=256, rb=32, lc=512):
    S, C = hglu.shape
    ts = min(ts, S)
    n = S // ts
    hb = ts // HALO_ROWS
    last_halo = S // HALO_ROWS - 1
    vec = pl.BlockSpec((1, C), lambda i: (0, 0))
    return pl.pallas_call(
        functools.partial(_conv_kernel, rb=rb, lc=lc),
        grid=(n,),
        in_specs=[pl.BlockSpec((HALO_ROWS, C), lambda i: (jnp.maximum(i * hb - 1, 0), 0)),
                  pl.BlockSpec((ts, C), lambda i: (i, 0)),
                  pl.BlockSpec((HALO_ROWS, C), lambda i: (jnp.minimum((i + 1) * hb, last_halo), 0)),
                  pl.BlockSpec((CONV_WIDTH, C), lambda i: (0, 0)),
                  vec, vec, vec],
        out_specs=pl.BlockSpec((ts, C), lambda i: (i, 0)),
        out_shape=jax.ShapeDtypeStruct((S, C), BF16),
        scratch_shapes=[pltpu.VMEM((ts + 2 * HALO_ROWS, C), F32),
                        pltpu.VMEM((SUBLANES - 1, ts + 2 * HALO_ROWS, lc), F32),
                        pltpu.VMEM((ts, C), F32)],
        compiler_params=_params("parallel"),
        name="conv",
    )(hglu, hglu, hglu, w, b, g, beta)


def _merge_kernel(hn_ref, hc_ref, o_ref, wgc_ref, wgm_ref, wpw_ref, wo_ref, out_ref):
    hn = hn_ref[...]
    gc = jax.nn.sigmoid(jnp.dot(hn, wgc_ref[...], preferred_element_type=F32))
    gm = jax.nn.sigmoid(jnp.dot(hn, wgm_ref[...], preferred_element_type=F32))
    yc = jnp.dot(hc_ref[...], wpw_ref[...], preferred_element_type=F32)
    ym = jnp.dot(o_ref[...], wo_ref[...], preferred_element_type=F32)
    out_ref[...] = (gc * yc + gm * ym).astype(BF16)


def _merge(hn, hc, o, wgc, wgm, wpw, wo, *, tm=512, tn=512):
    S, D = hn.shape
    tm = min(tm, S)
    row = pl.BlockSpec((tm, D), lambda i, j: (i, 0))
    col = pl.BlockSpec((D, tn), lambda i, j: (0, j))
    return pl.pallas_call(
        _merge_kernel,
        grid=(S // tm, D // tn),
        in_specs=[row, row, row, col, col, col, col],
        out_specs=pl.BlockSpec((tm, tn), lambda i, j: (i, j)),
        out_shape=jax.ShapeDtypeStruct((S, D), BF16),
        compiler_params=_params("parallel", "arbitrary"),
        name="merge",
    )(hn, hc, o, wgc, wgm, wpw, wo)


def _outproj_kernel(x_ref, m_ref, w_ref, o_ref):
    o_ref[...] = x_ref[...] + jnp.dot(m_ref[...], w_ref[...], preferred_element_type=F32)


def _out_proj(x, merged, w, *, tm=512):
    S, D = x.shape
    tm = min(tm, S)
    row = pl.BlockSpec((tm, D), lambda i: (i, 0))
    return pl.pallas_call(
        _outproj_kernel,
        grid=(S // tm,),
        in_specs=[row, row, pl.BlockSpec((D, D), lambda i: (0, 0))],
        out_specs=row,
        out_shape=jax.ShapeDtypeStruct((S, D), F32),
        compiler_params=_params("parallel"),
        name="out_proj",
    )(x, merged, w)


def _rot_half_cols(w):
    half = QK_ROPE_DIM // 2
    return jnp.concatenate([-w[..., half:], w[..., :half]], axis=-1)


def _pad_cols(w, n):
    return jnp.pad(w, [(0, 0)] * (w.ndim - 1) + [(0, n - w.shape[-1])])


def _layer(x, pos, freq, ffn1_norm_g, ffn1_w_gate, ffn1_w_up, ffn1_w_down, mix_norm_g, w_in,
           conv_w_dw, conv_b_dw, conv_ln_g, conv_ln_b, conv_w_pw_out, mla_q_norm_g, mla_w_uq,
           mla_kv_norm_g, mla_w_ukv, mla_w_o, w_out, ffn2_norm_g, ffn2_w_gate, ffn2_w_up,
           ffn2_w_down, out_norm_g, *, final):
    D = x.shape[1]
    C = conv_w_dw.shape[1]
    row = lambda v: v.reshape(1, -1)
    bf = lambda w: w.astype(BF16)

    o_q = 2 * C
    o_kv = o_q + Q_LORA_RANK
    o_kr = o_kv + KV_LORA_RANK
    o_gc = o_kr + QK_ROPE_DIM
    o_gm = o_gc + D
    w_glu = bf(w_in[:, :o_q])
    w_kr = w_in[:, o_kr:o_gc]
    w_small = bf(jnp.concatenate(
        [w_in[:, o_q:o_kr], _pad_cols(w_kr, LANES), _pad_cols(_rot_half_cols(w_kr), LANES)], axis=1))
    w_gc = bf(w_in[:, o_gc:o_gm])
    w_gm = bf(w_in[:, o_gm:])

    wq = mla_w_uq.reshape(Q_LORA_RANK, N_HEADS, QK_NOPE_DIM + QK_ROPE_DIM)
    wq_rope = wq[..., QK_NOPE_DIM:]
    wq = jnp.concatenate([wq[..., :QK_NOPE_DIM], _pad_cols(wq_rope, LANES),
                          _pad_cols(_rot_half_cols(wq_rope), LANES)], axis=-1)
    wqt = bf(wq.transpose(1, 2, 0))
    wkv = mla_w_ukv.reshape(KV_LORA_RANK, N_HEADS, QK_NOPE_DIM + V_HEAD_DIM)
    wk = bf(wkv[..., :QK_NOPE_DIM].transpose(1, 0, 2))
    wvt = bf(wkv[..., QK_NOPE_DIM:].transpose(1, 2, 0))

    x1, hn = _ffn(x, row(ffn1_norm_g), bf(ffn1_w_gate), bf(ffn1_w_up), bf(ffn1_w_down),
                  row(mix_norm_g), final=False)

    hglu = _glu_proj(hn, w_glu)
    hc = _conv(hglu, conv_w_dw, row(conv_b_dw), row(conv_ln_g), row(conv_ln_b))

    cq, ckv, kr, cst, snt = _small_proj(hn, w_small, row(mla_q_norm_g), row(mla_kv_norm_g), pos, freq)
    q_scale = (QK_NOPE_DIM + QK_ROPE_DIM) ** -0.5 * math.log2(math.e)
    qt, k, vt = _qkv_proj(cq, ckv, kr, cst, snt, wqt, wk, wvt, q_scale=q_scale)
    o = _attention(qt, k, vt)

    merged = _merge(hn, hc, o, w_gc, w_gm, bf(conv_w_pw_out), bf(mla_w_o))
    x2 = _out_proj(x1, merged, bf(w_out))
    (y,) = _ffn(x2, row(ffn2_norm_g), bf(ffn2_w_gate), bf(ffn2_w_up), bf(ffn2_w_down),
                row(out_norm_g), final=final)
    return y


def kernel(x, positions, ffn1_norm_g, ffn1_w_gate, ffn1_w_up, ffn1_w_down, mix_norm_g, w_in, conv_w_dw, conv_b_dw, conv_ln_g, conv_ln_b, conv_w_pw_out, mla_q_norm_g, mla_w_uq, mla_kv_norm_g, mla_w_ukv, mla_w_o, w_out, ffn2_norm_g, ffn2_w_gate, ffn2_w_up, ffn2_w_down, final_norm_g):
    B, S, D = x.shape
    depth = ffn1_norm_g.shape[0]
    assert depth == 1, "the fused final norm assumes a single layer"
    inv_freq = ROPE_THETA ** (-jnp.arange(0, QK_ROPE_DIM, 2, dtype=F32) / QK_ROPE_DIM)
    freq = _pad_cols(jnp.concatenate([inv_freq, inv_freq]), LANES).reshape(1, LANES)
    outs = []
    for b in range(B):
        y = _layer(x[b], positions[b].reshape(S, 1), freq,
                   ffn1_norm_g[0], ffn1_w_gate[0], ffn1_w_up[0], ffn1_w_down[0], mix_norm_g[0],
                   w_in[0], conv_w_dw[0], conv_b_dw[0], conv_ln_g[0], conv_ln_b[0],
                   conv_w_pw_out[0], mla_q_norm_g[0], mla_w_uq[0], mla_kv_norm_g[0], mla_w_ukv[0],
                   mla_w_o[0], w_out[0], ffn2_norm_g[0], ffn2_w_gate[0], ffn2_w_up[0],
                   ffn2_w_down[0], final_norm_g, final=True)
        outs.append(y)
    return jnp.stack(outs)
```

```python
import functools
import math

import jax
import jax.numpy as jnp
from jax import lax
from jax.experimental import pallas as pl
from jax.experimental.pallas import tpu as pltpu

N_HEADS = 16
QK_NOPE_DIM = 128
QK_ROPE_DIM = 64
V_HEAD_DIM = 128
Q_LORA_RANK = 768
KV_LORA_RANK = 512
CONV_WIDTH = 31
CONV_PAD = CONV_WIDTH // 2
ROPE_THETA = 10000.0
NORM_EPS = 1e-6

LANES = 128
SUBLANES = 8
QK_PAD_DIM = 256
HALO_ROWS = 16
VMEM_LIMIT = 56 * 1024 * 1024

F32 = jnp.float32
BF16 = jnp.bfloat16


def _params(*sem):
    return pltpu.CompilerParams(dimension_semantics=sem, vmem_limit_bytes=VMEM_LIMIT)


def _rms(x, g):
    return x * lax.rsqrt(jnp.mean(x * x, axis=-1, keepdims=True) + NORM_EPS) * g


def _ffn_kernel(x_ref, g_ref, wg_ref, wu_ref, wd_ref, g2_ref, *refs, final):
    if final:
        o_ref, hn_scr, acc_scr = refs
    else:
        o_ref, hn_out_ref, hn_scr, acc_scr = refs
    j = pl.program_id(1)

    @pl.when(j == 0)
    def _():
        hn_scr[...] = _rms(x_ref[...], g_ref[...]).astype(BF16)
        acc_scr[...] = jnp.zeros_like(acc_scr)

    h = hn_scr[...]
    a = jnp.dot(h, wg_ref[...], preferred_element_type=F32)
    u = jnp.dot(h, wu_ref[...], preferred_element_type=F32)
    act = (a * jax.nn.sigmoid(a) * u).astype(BF16)
    acc_scr[...] += jnp.dot(act, wd_ref[...], preferred_element_type=F32)

    @pl.when(j == pl.num_programs(1) - 1)
    def _():
        y = x_ref[...] + 0.5 * acc_scr[...]
        if final:
            o_ref[...] = _rms(y, g2_ref[...])
        else:
            o_ref[...] = y
            hn_out_ref[...] = _rms(y, g2_ref[...]).astype(BF16)


def _ffn(x, g, wg, wu, wd, g2, *, final, tm=512, tf=512):
    S, D = x.shape
    F = wg.shape[1]
    tm = min(tm, S)
    row = pl.BlockSpec((tm, D), lambda i, j: (i, 0))
    vec = pl.BlockSpec((1, D), lambda i, j: (0, 0))
    out_shape = [jax.ShapeDtypeStruct((S, D), F32)]
    out_specs = [row]
    if not final:
        out_shape.append(jax.ShapeDtypeStruct((S, D), BF16))
        out_specs.append(row)
    return pl.pallas_call(
        functools.partial(_ffn_kernel, final=final),
        grid=(S // tm, F // tf),
        in_specs=[row, vec,
                  pl.BlockSpec((D, tf), lambda i, j: (0, j)),
                  pl.BlockSpec((D, tf), lambda i, j: (0, j)),
                  pl.BlockSpec((tf, D), lambda i, j: (j, 0)),
                  vec],
        out_specs=out_specs,
        out_shape=out_shape,
        scratch_shapes=[pltpu.VMEM((tm, D), BF16), pltpu.VMEM((tm, D), F32)],
        compiler_params=_params("parallel", "arbitrary"),
        name="ffn_final" if final else "ffn_mix",
    )(x, g, wg, wu, wd, g2)


def _glu_kernel(h_ref, wa_ref, wg_ref, o_ref):
    h = h_ref[...]
    a = jnp.dot(h, wa_ref[...], preferred_element_type=F32)
    g = jnp.dot(h, wg_ref[...], preferred_element_type=F32)
    o_ref[...] = a * jax.nn.sigmoid(g)


def _glu_proj(hn, w_glu, *, tm=1024, tn=512):
    S, D = hn.shape
    C = w_glu.shape[1] // 2
    tm = min(tm, S)
    nj = C // tn
    return pl.pallas_call(
        _glu_kernel,
        grid=(S // tm, nj),
        in_specs=[pl.BlockSpec((tm, D), lambda i, j: (i, 0)),
                  pl.BlockSpec((D, tn), lambda i, j: (0, j)),
                  pl.BlockSpec((D, tn), lambda i, j: (0, j + nj))],
        out_specs=pl.BlockSpec((tm, tn), lambda i, j: (i, j)),
        out_shape=jax.ShapeDtypeStruct((S, C), F32),
        compiler_params=_params("parallel", "arbitrary"),
        name="glu_proj",
    )(hn, w_glu, w_glu)


def _small_kernel(h_ref, w_ref, gq_ref, gkv_ref, pos_col_ref, freq_row_ref, pos_row_ref, freq_col_ref,
                  cq_ref, ckv_ref, kr_ref, cst_ref, snt_ref):
    r = jnp.dot(h_ref[...], w_ref[...], preferred_element_type=F32)
    cq_ref[...] = _rms(r[:, :Q_LORA_RANK], gq_ref[...]).astype(BF16)
    c0 = Q_LORA_RANK
    ckv_ref[...] = _rms(r[:, c0:c0 + KV_LORA_RANK], gkv_ref[...]).astype(BF16)
    c1 = c0 + KV_LORA_RANK
    ang = pos_col_ref[...].astype(F32) * freq_row_ref[...]
    kr = r[:, c1:c1 + LANES] * jnp.cos(ang) + r[:, c1 + LANES:c1 + 2 * LANES] * jnp.sin(ang)
    kr_ref[...] = kr.astype(BF16)
    ang_t = freq_col_ref[...] * pos_row_ref[...].astype(F32)
    cst_ref[...] = jnp.cos(ang_t)
    snt_ref[...] = jnp.sin(ang_t)


def _small_proj(hn, w_small, gq, gkv, pos, freq, *, tm=512):
    S, D = hn.shape
    tm = min(tm, S)
    row = lambda n: pl.BlockSpec((tm, n), lambda i: (i, 0))
    col = lambda n: pl.BlockSpec((n, tm), lambda i: (0, i))
    full = lambda a: pl.BlockSpec(a.shape, lambda i: (0, 0))
    pos_col, pos_row = pos.reshape(S, 1), pos.reshape(1, S)
    freq_row, freq_col = freq.reshape(1, LANES), freq.reshape(LANES, 1)[:QK_ROPE_DIM]
    return pl.pallas_call(
        _small_kernel,
        grid=(S // tm,),
        in_specs=[row(D), full(w_small), full(gq), full(gkv), row(1), full(freq_row), col(1), full(freq_col)],
        out_specs=[row(Q_LORA_RANK), row(KV_LORA_RANK), row(LANES), col(QK_ROPE_DIM), col(QK_ROPE_DIM)],
        out_shape=[jax.ShapeDtypeStruct((S, Q_LORA_RANK), BF16),
                   jax.ShapeDtypeStruct((S, KV_LORA_RANK), BF16),
                   jax.ShapeDtypeStruct((S, LANES), BF16),
                   jax.ShapeDtypeStruct((QK_ROPE_DIM, S), F32),
                   jax.ShapeDtypeStruct((QK_ROPE_DIM, S), F32)],
        compiler_params=_params("parallel"),
        name="small_proj",
    )(hn, w_small, gq, gkv, pos_col, freq_row, pos_row, freq_col)


_NT = (((1,), (1,)), ((), ()))


def _qkv_kernel(cq_ref, ckv_ref, kr_ref, cst_ref, snt_ref, wqt_ref, wk_ref, wvt_ref,
                qt_ref, k_ref, vt_ref, *, q_scale):
    cq = cq_ref[...]
    ckv = ckv_ref[...]
    kr = kr_ref[...]
    cst = cst_ref[...]
    snt = snt_ref[...]
    for h in range(N_HEADS):
        rt = lax.dot_general(wqt_ref[h], cq, _NT, preferred_element_type=F32)
        n0, n1, n2 = QK_NOPE_DIM, QK_NOPE_DIM + QK_ROPE_DIM, QK_NOPE_DIM + 2 * QK_ROPE_DIM
        qt_ref[h, 0:n0, :] = (rt[0:n0] * q_scale).astype(BF16)
        q_rope = rt[n0:n1] * cst + rt[n1:n2] * snt
        qt_ref[h, n0:n1, :] = (q_rope * q_scale).astype(BF16)
        qt_ref[h, n1:QK_PAD_DIM, :] = jnp.zeros((QK_PAD_DIM - n1, rt.shape[1]), BF16)
        k_nope = jnp.dot(ckv, wk_ref[h], preferred_element_type=F32)
        k_ref[h, :, 0:LANES] = k_nope.astype(BF16)
        k_ref[h, :, LANES:2 * LANES] = kr
        vt = lax.dot_general(wvt_ref[h], ckv, _NT, preferred_element_type=F32)
        vt_ref[h] = vt.astype(BF16)


def _qkv_proj(cq, ckv, kr, cst, snt, wqt, wk, wvt, *, q_scale, tm=256):
    S = cq.shape[0]
    tm = min(tm, S)
    row = lambda n: pl.BlockSpec((tm, n), lambda i: (i, 0))
    col = lambda n: pl.BlockSpec((n, tm), lambda i: (0, i))
    full3 = lambda a: pl.BlockSpec(a.shape, lambda i: (0, 0, 0))
    return pl.pallas_call(
        functools.partial(_qkv_kernel, q_scale=q_scale),
        grid=(S // tm,),
        in_specs=[row(Q_LORA_RANK), row(KV_LORA_RANK), row(LANES), col(QK_ROPE_DIM), col(QK_ROPE_DIM),
                  full3(wqt), full3(wk), full3(wvt)],
        out_specs=[pl.BlockSpec((N_HEADS, QK_PAD_DIM, tm), lambda i: (0, 0, i)),
                   pl.BlockSpec((N_HEADS, tm, QK_PAD_DIM), lambda i: (0, i, 0)),
                   pl.BlockSpec((N_HEADS, V_HEAD_DIM, tm), lambda i: (0, 0, i))],
        out_shape=[jax.ShapeDtypeStruct((N_HEADS, QK_PAD_DIM, S), BF16),
                   jax.ShapeDtypeStruct((N_HEADS, S, QK_PAD_DIM), BF16),
                   jax.ShapeDtypeStruct((N_HEADS, V_HEAD_DIM, S), BF16)],
        compiler_params=_params("parallel"),
        name="qkv_proj",
    )(cq, ckv, kr, cst, snt, wqt, wk, wvt)


def _attn_kernel(qt_ref, k_ref, vt_ref, o_ref, s_scr, *, tk, unroll):
    qt = qt_ref[...]
    tq = qt.shape[1]
    n_kv = k_ref.shape[0] // tk

    def scores(t, slot):
        start = pl.multiple_of(t * tk, tk)
        s = jnp.dot(k_ref[pl.ds(start, tk), :], qt, preferred_element_type=F32)
        s_scr[slot] = s
        return jnp.max(s, axis=0, keepdims=True)

    def softmax_pv(slot, s_max, t, m, l, acc):
        m_new = jnp.maximum(m, s_max)
        alpha = jnp.exp2(m - m_new)
        p = jnp.exp2(s_scr[slot] - m_new)
        l = alpha * l + jnp.sum(p, axis=0, keepdims=True)
        start = pl.multiple_of(t * tk, tk)
        pv = jnp.dot(vt_ref[:, pl.ds(start, tk)], p.astype(BF16), preferred_element_type=F32)
        return m_new, l, alpha * acc + pv

    def body(u, carry):
        m, l, acc, s_max = carry
        t = unroll * u
        for j in range(unroll):
            nxt = t + j + 1
            if j == unroll - 1:
                nxt = jnp.minimum(nxt, n_kv - 1)
            next_max = scores(nxt, (j + 1) % 2)
            m, l, acc = softmax_pv(j % 2, s_max, t + j, m, l, acc)
            s_max = next_max
        return m, l, acc, s_max

    m0 = jnp.full((1, tq), -jnp.inf, F32)
    l0 = jnp.zeros((1, tq), F32)
    acc0 = jnp.zeros((V_HEAD_DIM, tq), F32)
    _, l, acc, _ = lax.fori_loop(0, n_kv // unroll, body, (m0, l0, acc0, scores(0, 0)))
    o_ref[...] = (acc / l).T.astype(BF16)


def _attention(qt, k, vt, *, tq=512, tk=512, unroll=16):
    H, S, _ = k.shape
    tq = min(tq, S)
    tk = min(tk, S // unroll)
    assert unroll % 2 == 0 and S % (unroll * tk) == 0 and S % tq == 0
    return pl.pallas_call(
        functools.partial(_attn_kernel, tk=tk, unroll=unroll),
        grid=(H, S // tq),
        in_specs=[pl.BlockSpec((None, QK_PAD_DIM, tq), lambda h, i: (h, 0, i)),
                  pl.BlockSpec((None, S, QK_PAD_DIM), lambda h, i: (h, 0, 0)),
                  pl.BlockSpec((None, V_HEAD_DIM, S), lambda h, i: (h, 0, 0))],
        out_specs=pl.BlockSpec((tq, V_HEAD_DIM), lambda h, i: (i, h)),
        out_shape=jax.ShapeDtypeStruct((S, H * V_HEAD_DIM), BF16),
        scratch_shapes=[pltpu.VMEM((2, tk, tq), F32)],
        compiler_params=_params("parallel", "arbitrary"),
        name="attention",
    )(qt, k, vt)


def _conv_kernel(prev_ref, cur_ref, next_ref, w_ref, b_ref, g_ref, beta_ref, o_ref,
                 buf, shifted, conv_scr, *, rb, lc):
    i = pl.program_id(0)
    ts, C = cur_ref.shape
    first = i == 0
    last = i == pl.num_programs(0) - 1
    buf[0:HALO_ROWS, :] = jnp.where(first, 0.0, prev_ref[...])
    buf[HALO_ROWS:HALO_ROWS + ts, :] = cur_ref[...]
    buf[HALO_ROWS + ts:, :] = jnp.where(last, 0.0, next_ref[...])
    off = HALO_ROWS - CONV_PAD
    n_sh = ts + SUBLANES * ((CONV_WIDTH + off - 1) // SUBLANES)

    for c0 in range(0, C, lc):
        for b in range(1, SUBLANES):
            shifted[b - 1, 0:n_sh, :] = buf[b:b + n_sh, c0:c0 + lc]

        def rows(rblk, _, c0=c0):
            r0 = pl.multiple_of(rblk * rb, rb)
            acc = jnp.broadcast_to(b_ref[:, c0:c0 + lc], (rb, lc))
            for k in range(CONV_WIDTH):
                a, b = divmod(k + off, SUBLANES)
                start = pl.multiple_of(r0 + a * SUBLANES, SUBLANES)
                if b == 0:
                    tap = buf[pl.ds(start, rb), c0:c0 + lc]
                else:
                    tap = shifted[b - 1, pl.ds(start, rb), :]
                acc = acc + w_ref[k:k + 1, c0:c0 + lc] * tap
            conv_scr[pl.ds(r0, rb), c0:c0 + lc] = acc
            return 0
        lax.fori_loop(0, ts // rb, rows, 0)

    y = conv_scr[...]
    mu = jnp.mean(y, axis=-1, keepdims=True)
    yc = y - mu
    var = jnp.mean(yc * yc, axis=-1, keepdims=True)
    z = yc * lax.rsqrt(var + NORM_EPS) * g_ref[...] + beta_ref[...]
    o_ref[...] = (z * jax.nn.sigmoid(z)).astype(BF16)


def _conv(hglu, w, b, g, beta, *, ts=256, rb=32, lc=512):
    S, C = hglu.shape
    ts = min(ts, S)
    n = S // ts
    hb = ts // HALO_ROWS
    last_halo = S // HALO_ROWS - 1
    vec = pl.BlockSpec((1, C), lambda i: (0, 0))
    return pl.pallas_call(
        functools.partial(_conv_kernel, rb=rb, lc=lc),
        grid=(n,),
        in_specs=[pl.BlockSpec((HALO_ROWS, C), lambda i: (jnp.maximum(i * hb - 1, 0), 0)),
                  pl.BlockSpec((ts, C), lambda i: (i, 0)),
                  pl.BlockSpec((HALO_ROWS, C), lambda i: (jnp.minimum((i + 1) * hb, last_halo), 0)),
                  pl.BlockSpec((CONV_WIDTH, C), lambda i: (0, 0)),
                  vec, vec, vec],
        out_specs=pl.BlockSpec((ts, C), lambda i: (i, 0)),
        out_shape=jax.ShapeDtypeStruct((S, C), BF16),
        scratch_shapes=[pltpu.VMEM((ts + 2 * HALO_ROWS, C), F32),
                        pltpu.VMEM((SUBLANES - 1, ts + 2 * HALO_ROWS, lc), F32),
                        pltpu.VMEM((ts, C), F32)],
        compiler_params=_params("parallel"),
        name="conv",
    )(hglu, hglu, hglu, w, b, g, beta)


def _merge_kernel(hn_ref, hc_ref, o_ref, wgc_ref, wgm_ref, wpw_ref, wo_ref, out_ref):
    hn = hn_ref[...]
    gc = jax.nn.sigmoid(jnp.dot(hn, wgc_ref[...], preferred_element_type=F32))
    gm = jax.nn.sigmoid(jnp.dot(hn, wgm_ref[...], preferred_element_type=F32))
    yc = jnp.dot(hc_ref[...], wpw_ref[...], preferred_element_type=F32)
    ym = jnp.dot(o_ref[...], wo_ref[...], preferred_element_type=F32)
    out_ref[...] = (gc * yc + gm * ym).astype(BF16)


def _merge(hn, hc, o, wgc, wgm, wpw, wo, *, tm=512, tn=512):
    S, D = hn.shape
    tm = min(tm, S)
    row = pl.BlockSpec((tm, D), lambda i, j: (i, 0))
    col = pl.BlockSpec((D, tn), lambda i, j: (0, j))
    return pl.pallas_call(
        _merge_kernel,
        grid=(S // tm, D // tn),
        in_specs=[row, row, row, col, col, col, col],
        out_specs=pl.BlockSpec((tm, tn), lambda i, j: (i, j)),
        out_shape=jax.ShapeDtypeStruct((S, D), BF16),
        compiler_params=_params("parallel", "arbitrary"),
        name="merge",
    )(hn, hc, o, wgc, wgm, wpw, wo)


def _outproj_kernel(x_ref, m_ref, w_ref, o_ref):
    o_ref[...] = x_ref[...] + jnp.dot(m_ref[...], w_ref[...], preferred_element_type=F32)


def _out_proj(x, merged, w, *, tm=512):
    S, D = x.shape
    tm = min(tm, S)
    row = pl.BlockSpec((tm, D), lambda i: (i, 0))
    return pl.pallas_call(
        _outproj_kernel,
        grid=(S // tm,),
        in_specs=[row, row, pl.BlockSpec((D, D), lambda i: (0, 0))],
        out_specs=row,
        out_shape=jax.ShapeDtypeStruct((S, D), F32),
        compiler_params=_params("parallel"),
        name="out_proj",
    )(x, merged, w)


def _rot_half_cols(w):
    half = QK_ROPE_DIM // 2
    return jnp.concatenate([-w[..., half:], w[..., :half]], axis=-1)


def _pad_cols(w, n):
    return jnp.pad(w, [(0, 0)] * (w.ndim - 1) + [(0, n - w.shape[-1])])


def _layer(x, pos, freq, ffn1_norm_g, ffn1_w_gate, ffn1_w_up, ffn1_w_down, mix_norm_g, w_in,
           conv_w_dw, conv_b_dw, conv_ln_g, conv_ln_b, conv_w_pw_out, mla_q_norm_g, mla_w_uq,
           mla_kv_norm_g, mla_w_ukv, mla_w_o, w_out, ffn2_norm_g, ffn2_w_gate, ffn2_w_up,
           ffn2_w_down, out_norm_g, *, final):
    D = x.shape[1]
    C = conv_w_dw.shape[1]
    row = lambda v: v.reshape(1, -1)
    bf = lambda w: w.astype(BF16)

    o_q = 2 * C
    o_kv = o_q + Q_LORA_RANK
    o_kr = o_kv + KV_LORA_RANK
    o_gc = o_kr + QK_ROPE_DIM
    o_gm = o_gc + D
    w_glu = bf(w_in[:, :o_q])
    w_kr = w_in[:, o_kr:o_gc]
    w_small = bf(jnp.concatenate(
        [w_in[:, o_q:o_kr], _pad_cols(w_kr, LANES), _pad_cols(_rot_half_cols(w_kr), LANES)], axis=1))
    w_gc = bf(w_in[:, o_gc:o_gm])
    w_gm = bf(w_in[:, o_gm:])

    wq = mla_w_uq.reshape(Q_LORA_RANK, N_HEADS, QK_NOPE_DIM + QK_ROPE_DIM)
    wq_rope = wq[..., QK_NOPE_DIM:]
    wq = jnp.concatenate([wq[..., :QK_NOPE_DIM], wq_rope, _rot_half_cols(wq_rope)], axis=-1)
    wqt = bf(wq.transpose(1, 2, 0))
    wkv = mla_w_ukv.reshape(KV_LORA_RANK, N_HEADS, QK_NOPE_DIM + V_HEAD_DIM)
    wk = bf(wkv[..., :QK_NOPE_DIM].transpose(1, 0, 2))
    wvt = bf(wkv[..., QK_NOPE_DIM:].transpose(1, 2, 0))

    x1, hn = _ffn(x, row(ffn1_norm_g), bf(ffn1_w_gate), bf(ffn1_w_up), bf(ffn1_w_down),
                  row(mix_norm_g), final=False)

    hglu = _glu_proj(hn, w_glu)
    hc = _conv(hglu, conv_w_dw, row(conv_b_dw), row(conv_ln_g), row(conv_ln_b))

    cq, ckv, kr, cst, snt = _small_proj(hn, w_small, row(mla_q_norm_g), row(mla_kv_norm_g), pos, freq)
    q_scale = (QK_NOPE_DIM + QK_ROPE_DIM) ** -0.5 * math.log2(math.e)
    qt, k, vt = _qkv_proj(cq, ckv, kr, cst, snt, wqt, wk, wvt, q_scale=q_scale)
    o = _attention(qt, k, vt)

    merged = _merge(hn, hc, o, w_gc, w_gm, bf(conv_w_pw_out), bf(mla_w_o))
    x2 = _out_proj(x1, merged, bf(w_out))
    (y,) = _ffn(x2, row(ffn2_norm_g), bf(ffn2_w_gate), bf(ffn2_w_up), bf(ffn2_w_down),
                row(out_norm_g), final=final)
    return y


def kernel(x, positions, ffn1_norm_g, ffn1_w_gate, ffn1_w_up, ffn1_w_down, mix_norm_g, w_in, conv_w_dw, conv_b_dw, conv_ln_g, conv_ln_b, conv_w_pw_out, mla_q_norm_g, mla_w_uq, mla_kv_norm_g, mla_w_ukv, mla_w_o, w_out, ffn2_norm_g, ffn2_w_gate, ffn2_w_up, ffn2_w_down, final_norm_g):
    B, S, D = x.shape
    depth = ffn1_norm_g.shape[0]
    assert depth == 1, "the fused final norm assumes a single layer"
    inv_freq = ROPE_THETA ** (-jnp.arange(0, QK_ROPE_DIM, 2, dtype=F32) / QK_ROPE_DIM)
    freq = _pad_cols(jnp.concatenate([inv_freq, inv_freq]), LANES).reshape(1, LANES)
    outs = []
    for b in range(B):
        y = _layer(x[b], positions[b].reshape(S, 1), freq,
                   ffn1_norm_g[0], ffn1_w_gate[0], ffn1_w_up[0], ffn1_w_down[0], mix_norm_g[0],
                   w_in[0], conv_w_dw[0], conv_b_dw[0], conv_ln_g[0], conv_ln_b[0],
                   conv_w_pw_out[0], mla_q_norm_g[0], mla_w_uq[0], mla_kv_norm_g[0], mla_w_ukv[0],
                   mla_w_o[0], w_out[0], ffn2_norm_g[0], ffn2_w_gate[0], ffn2_w_up[0],
                   ffn2_w_down[0], final_norm_g, final=True)
        outs.append(y)
    return jnp.stack(outs)
```

```python
import functools
import math

import jax
import jax.numpy as jnp
from jax import lax
from jax.experimental import pallas as pl
from jax.experimental.pallas import tpu as pltpu

N_HEADS = 16
QK_NOPE_DIM = 128
QK_ROPE_DIM = 64
V_HEAD_DIM = 128
Q_LORA_RANK = 768
KV_LORA_RANK = 512
CONV_WIDTH = 31
CONV_PAD = CONV_WIDTH // 2
ROPE_THETA = 10000.0
NORM_EPS = 1e-6

LANES = 128
SUBLANES = 8
QK_PAD_DIM = 256
QK_DIM = QK_NOPE_DIM + QK_ROPE_DIM
V_EXT_DIM = V_HEAD_DIM + 16
BOUND_MARGIN = 1.0 + 2.0 ** -6
L_MIN = 2.0 ** -80
HALO_ROWS = 16
VMEM_LIMIT = 56 * 1024 * 1024

F32 = jnp.float32
BF16 = jnp.bfloat16


def _params(*sem):
    return pltpu.CompilerParams(dimension_semantics=sem, vmem_limit_bytes=VMEM_LIMIT)


def _rms(x, g):
    return x * lax.rsqrt(jnp.mean(x * x, axis=-1, keepdims=True) + NORM_EPS) * g


def _ffn_kernel(x_ref, g_ref, wg_ref, wu_ref, wd_ref, g2_ref, *refs, final):
    if final:
        o_ref, hn_scr, acc_scr = refs
    else:
        o_ref, hn_out_ref, hn_scr, acc_scr = refs
    j = pl.program_id(1)

    @pl.when(j == 0)
    def _():
        hn_scr[...] = _rms(x_ref[...], g_ref[...]).astype(BF16)
        acc_scr[...] = jnp.zeros_like(acc_scr)

    h = hn_scr[...]
    a = jnp.dot(h, wg_ref[...], preferred_element_type=F32)
    u = jnp.dot(h, wu_ref[...], preferred_element_type=F32)
    act = (a * jax.nn.sigmoid(a) * u).astype(BF16)
    acc_scr[...] += jnp.dot(act, wd_ref[...], preferred_element_type=F32)

    @pl.when(j == pl.num_programs(1) - 1)
    def _():
        y = x_ref[...] + 0.5 * acc_scr[...]
        if final:
            o_ref[...] = _rms(y, g2_ref[...])
        else:
            o_ref[...] = y
            hn_out_ref[...] = _rms(y, g2_ref[...]).astype(BF16)


def _ffn(x, g, wg, wu, wd, g2, *, final, tm=512, tf=512):
    S, D = x.shape
    F = wg.shape[1]
    tm = min(tm, S)
    row = pl.BlockSpec((tm, D), lambda i, j: (i, 0))
    vec = pl.BlockSpec((1, D), lambda i, j: (0, 0))
    out_shape = [jax.ShapeDtypeStruct((S, D), F32)]
    out_specs = [row]
    if not final:
        out_shape.append(jax.ShapeDtypeStruct((S, D), BF16))
        out_specs.append(row)
    return pl.pallas_call(
        functools.partial(_ffn_kernel, final=final),
        grid=(S // tm, F // tf),
        in_specs=[row, vec,
                  pl.BlockSpec((D, tf), lambda i, j: (0, j)),
                  pl.BlockSpec((D, tf), lambda i, j: (0, j)),
                  pl.BlockSpec((tf, D), lambda i, j: (j, 0)),
                  vec],
        out_specs=out_specs,
        out_shape=out_shape,
        scratch_shapes=[pltpu.VMEM((tm, D), BF16), pltpu.VMEM((tm, D), F32)],
        compiler_params=_params("parallel", "arbitrary"),
        name="ffn_final" if final else "ffn_mix",
    )(x, g, wg, wu, wd, g2)


def _glu_kernel(h_ref, wa_ref, wg_ref, o_ref):
    h = h_ref[...]
    a = jnp.dot(h, wa_ref[...], preferred_element_type=F32)
    g = jnp.dot(h, wg_ref[...], preferred_element_type=F32)
    o_ref[...] = a * jax.nn.sigmoid(g)


def _glu_proj(hn, w_glu, *, tm=1024, tn=512):
    S, D = hn.shape
    C = w_glu.shape[1] // 2
    tm = min(tm, S)
    nj = C // tn
    return pl.pallas_call(
        _glu_kernel,
        grid=(S // tm, nj),
        in_specs=[pl.BlockSpec((tm, D), lambda i, j: (i, 0)),
                  pl.BlockSpec((D, tn), lambda i, j: (0, j)),
                  pl.BlockSpec((D, tn), lambda i, j: (0, j + nj))],
        out_specs=pl.BlockSpec((tm, tn), lambda i, j: (i, j)),
        out_shape=jax.ShapeDtypeStruct((S, C), F32),
        compiler_params=_params("parallel", "arbitrary"),
        name="glu_proj",
    )(hn, w_glu, w_glu)


def _small_kernel(h_ref, w_ref, gq_ref, gkv_ref, pos_col_ref, freq_row_ref, pos_row_ref, freq_col_ref,
                  cq_ref, ckv_ref, kr_ref, cst_ref, snt_ref):
    r = jnp.dot(h_ref[...], w_ref[...], preferred_element_type=F32)
    cq_ref[...] = _rms(r[:, :Q_LORA_RANK], gq_ref[...]).astype(BF16)
    c0 = Q_LORA_RANK
    ckv_ref[...] = _rms(r[:, c0:c0 + KV_LORA_RANK], gkv_ref[...]).astype(BF16)
    c1 = c0 + KV_LORA_RANK
    ang = pos_col_ref[...].astype(F32) * freq_row_ref[...]
    kr = r[:, c1:c1 + LANES] * jnp.cos(ang) + r[:, c1 + LANES:c1 + 2 * LANES] * jnp.sin(ang)
    lane = lax.broadcasted_iota(jnp.int32, kr.shape, 1)
    kr_ref[...] = jnp.where(lane == QK_ROPE_DIM, 1.0, kr).astype(BF16)
    ang_t = freq_col_ref[...] * pos_row_ref[...].astype(F32)
    cst_ref[...] = jnp.cos(ang_t)
    snt_ref[...] = jnp.sin(ang_t)


def _small_proj(hn, w_small, gq, gkv, pos, freq, *, tm=512):
    S, D = hn.shape
    tm = min(tm, S)
    row = lambda n: pl.BlockSpec((tm, n), lambda i: (i, 0))
    col = lambda n: pl.BlockSpec((n, tm), lambda i: (0, i))
    full = lambda a: pl.BlockSpec(a.shape, lambda i: (0, 0))
    pos_col, pos_row = pos.reshape(S, 1), pos.reshape(1, S)
    freq_row, freq_col = freq.reshape(1, LANES), freq.reshape(LANES, 1)[:QK_ROPE_DIM]
    return pl.pallas_call(
        _small_kernel,
        grid=(S // tm,),
        in_specs=[row(D), full(w_small), full(gq), full(gkv), row(1), full(freq_row), col(1), full(freq_col)],
        out_specs=[row(Q_LORA_RANK), row(KV_LORA_RANK), row(LANES), col(QK_ROPE_DIM), col(QK_ROPE_DIM)],
        out_shape=[jax.ShapeDtypeStruct((S, Q_LORA_RANK), BF16),
                   jax.ShapeDtypeStruct((S, KV_LORA_RANK), BF16),
                   jax.ShapeDtypeStruct((S, LANES), BF16),
                   jax.ShapeDtypeStruct((QK_ROPE_DIM, S), F32),
                   jax.ShapeDtypeStruct((QK_ROPE_DIM, S), F32)],
        compiler_params=_params("parallel"),
        name="small_proj",
    )(hn, w_small, gq, gkv, pos_col, freq_row, pos_row, freq_col)


_NT = (((1,), (1,)), ((), ()))


def _sq(x):
    x = x.astype(F32)
    return x * x


def _qkv_kernel(cq_ref, ckv_ref, kr_ref, cst_ref, snt_ref, wqt_ref, wk_ref, wvt_ref,
                qt_ref, k_ref, vt_ref, qn2_ref, kmax2_ref, *, q_scale):
    @pl.when(pl.program_id(0) == 0)
    def _():
        kmax2_ref[...] = jnp.zeros_like(kmax2_ref)

    cq = cq_ref[...]
    ckv = ckv_ref[...]
    kr = kr_ref[...]
    cst = cst_ref[...]
    snt = snt_ref[...]
    kr_n2 = jnp.sum(_sq(kr), axis=1, keepdims=True) - 1.0
    for h in range(N_HEADS):
        rt = lax.dot_general(wqt_ref[h], cq, _NT, preferred_element_type=F32)
        n0, n1, n2 = QK_NOPE_DIM, QK_DIM, QK_DIM + QK_ROPE_DIM
        q_nope = (rt[0:n0] * q_scale).astype(BF16)
        q_rope = ((rt[n0:n1] * cst + rt[n1:n2] * snt) * q_scale).astype(BF16)
        qt_ref[h, 0:n0, :] = q_nope
        qt_ref[h, n0:n1, :] = q_rope
        qt_ref[h, n1:QK_PAD_DIM, :] = jnp.zeros((QK_PAD_DIM - n1, rt.shape[1]), BF16)
        k_nope = jnp.dot(ckv, wk_ref[h], preferred_element_type=F32).astype(BF16)
        k_ref[h, :, 0:LANES] = k_nope
        k_ref[h, :, LANES:2 * LANES] = kr
        vt = lax.dot_general(wvt_ref[h], ckv, _NT, preferred_element_type=F32)
        vt_ref[h, 0:V_HEAD_DIM, :] = vt.astype(BF16)
        vt_ref[h, V_HEAD_DIM:V_EXT_DIM, :] = jnp.ones((V_EXT_DIM - V_HEAD_DIM, vt.shape[1]), BF16)
        qn2_ref[h] = jnp.sum(_sq(q_nope), axis=0, keepdims=True) + jnp.sum(_sq(q_rope), axis=0, keepdims=True)
        k_n2 = jnp.sum(_sq(k_nope), axis=1, keepdims=True) + kr_n2
        kmax2_ref[h] = jnp.maximum(kmax2_ref[h], jnp.max(k_n2, axis=0, keepdims=True))


def _qkv_proj(cq, ckv, kr, cst, snt, wqt, wk, wvt, *, q_scale, tm=256):
    S = cq.shape[0]
    tm = min(tm, S)
    row = lambda n: pl.BlockSpec((tm, n), lambda i: (i, 0))
    col = lambda n: pl.BlockSpec((n, tm), lambda i: (0, i))
    full3 = lambda a: pl.BlockSpec(a.shape, lambda i: (0, 0, 0))
    return pl.pallas_call(
        functools.partial(_qkv_kernel, q_scale=q_scale),
        grid=(S // tm,),
        in_specs=[row(Q_LORA_RANK), row(KV_LORA_RANK), row(LANES), col(QK_ROPE_DIM), col(QK_ROPE_DIM),
                  full3(wqt), full3(wk), full3(wvt)],
        out_specs=[pl.BlockSpec((N_HEADS, QK_PAD_DIM, tm), lambda i: (0, 0, i)),
                   pl.BlockSpec((N_HEADS, tm, QK_PAD_DIM), lambda i: (0, i, 0)),
                   pl.BlockSpec((N_HEADS, V_EXT_DIM, tm), lambda i: (0, 0, i)),
                   pl.BlockSpec((N_HEADS, 1, tm), lambda i: (0, 0, i)),
                   pl.BlockSpec((N_HEADS, SUBLANES, LANES), lambda i: (0, 0, 0))],
        out_shape=[jax.ShapeDtypeStruct((N_HEADS, QK_PAD_DIM, S), BF16),
                   jax.ShapeDtypeStruct((N_HEADS, S, QK_PAD_DIM), BF16),
                   jax.ShapeDtypeStruct((N_HEADS, V_EXT_DIM, S), BF16),
                   jax.ShapeDtypeStruct((N_HEADS, 1, S), F32),
                   jax.ShapeDtypeStruct((N_HEADS, SUBLANES, LANES), F32)],
        compiler_params=_params("arbitrary"),
        name="qkv_proj",
    )(cq, ckv, kr, cst, snt, wqt, wk, wvt)


def _attn_bounded_kernel(qt_ref, qn2_ref, kmax2_ref, k_ref, vt_ref, o_ref, l_ref, *, tk, unroll):
    tq = qt_ref.shape[1]
    n_kv = k_ref.shape[0] // tk
    bound = jnp.sqrt(qn2_ref[...] * kmax2_ref[0:1, 0:1]) * BOUND_MARGIN
    tile = 2 * SUBLANES
    row = lax.broadcasted_iota(jnp.int32, (tile, tq), 0)
    offset_rows = jnp.where(row == 0, -bound, 0.0).astype(BF16)
    qt = jnp.concatenate([qt_ref[0:QK_DIM, :], offset_rows, qt_ref[QK_DIM + tile:, :]], axis=0)

    def body(u, acc):
        for j in range(unroll):
            start = pl.multiple_of((unroll * u + j) * tk, tk)
            s = jnp.dot(k_ref[pl.ds(start, tk), :], qt, preferred_element_type=F32)
            p = jnp.exp2(s).astype(BF16)
            acc = acc + jnp.dot(vt_ref[:, pl.ds(start, tk)], p, preferred_element_type=F32)
        return acc

    acc = lax.fori_loop(0, n_kv // unroll, body, jnp.zeros((V_EXT_DIM, tq), F32))
    l = acc[V_HEAD_DIM:V_HEAD_DIM + 1]
    l_ref[...] = l
    o_ref[...] = (acc[:V_HEAD_DIM] / l).T.astype(BF16)


def _attention_bounded(qt, qn2, kmax2, k, vt, *, tq=1024, tk=512, unroll=8):
    H, S, _ = k.shape
    tq = min(tq, S)
    tk = min(tk, S // unroll)
    assert S % (unroll * tk) == 0 and S % tq == 0
    return pl.pallas_call(
        functools.partial(_attn_bounded_kernel, tk=tk, unroll=unroll),
        grid=(H, S // tq),
        in_specs=[pl.BlockSpec((None, QK_PAD_DIM, tq), lambda h, i: (h, 0, i)),
                  pl.BlockSpec((None, 1, tq), lambda h, i: (h, 0, i)),
                  pl.BlockSpec((None, SUBLANES, LANES), lambda h, i: (h, 0, 0)),
                  pl.BlockSpec((None, S, QK_PAD_DIM), lambda h, i: (h, 0, 0)),
                  pl.BlockSpec((None, V_EXT_DIM, S), lambda h, i: (h, 0, 0))],
        out_specs=[pl.BlockSpec((tq, V_HEAD_DIM), lambda h, i: (i, h)),
                   pl.BlockSpec((None, 1, tq), lambda h, i: (h, 0, i))],
        out_shape=[jax.ShapeDtypeStruct((S, H * V_HEAD_DIM), BF16),
                   jax.ShapeDtypeStruct((H, 1, S), F32)],
        compiler_params=_params("parallel", "arbitrary"),
        name="attention_bounded",
    )(qt, qn2, kmax2, k, vt)


def _attn_kernel(qt_ref, k_ref, vt_ref, o_ref, s_scr, *, tk, unroll):
    qt = qt_ref[...]
    tq = qt.shape[1]
    n_kv = k_ref.shape[0] // tk

    def scores(t, slot):
        start = pl.multiple_of(t * tk, tk)
        s = jnp.dot(k_ref[pl.ds(start, tk), :], qt, preferred_element_type=F32)
        s_scr[slot] = s
        return jnp.max(s, axis=0, keepdims=True)

    def softmax_pv(slot, s_max, t, m, acc):
        m_new = jnp.maximum(m, s_max)
        alpha = jnp.exp2(m - m_new)
        p = jnp.exp2(s_scr[slot] - m_new)
        start = pl.multiple_of(t * tk, tk)
        pv = jnp.dot(vt_ref[:, pl.ds(start, tk)], p.astype(BF16), preferred_element_type=F32)
        return m_new, alpha * acc + pv

    def body(u, carry):
        m, acc, s_max = carry
        t = unroll * u
        for j in range(unroll):
            nxt = t + j + 1
            if j == unroll - 1:
                nxt = jnp.minimum(nxt, n_kv - 1)
            next_max = scores(nxt, (j + 1) % 2)
            m, acc = softmax_pv(j % 2, s_max, t + j, m, acc)
            s_max = next_max
        return m, acc, s_max

    m0 = jnp.full((1, tq), -jnp.inf, F32)
    acc0 = jnp.zeros((V_EXT_DIM, tq), F32)
    _, acc, _ = lax.fori_loop(0, n_kv // unroll, body, (m0, acc0, scores(0, 0)))
    o_ref[...] = (acc[:V_HEAD_DIM] / acc[V_HEAD_DIM:V_HEAD_DIM + 1]).T.astype(BF16)


def _attention(qt, k, vt, *, tq=512, tk=512, unroll=16):
    H, S, _ = k.shape
    tq = min(tq, S)
    tk = min(tk, S // unroll)
    assert unroll % 2 == 0 and S % (unroll * tk) == 0 and S % tq == 0
    return pl.pallas_call(
        functools.partial(_attn_kernel, tk=tk, unroll=unroll),
        grid=(H, S // tq),
        in_specs=[pl.BlockSpec((None, QK_PAD_DIM, tq), lambda h, i: (h, 0, i)),
                  pl.BlockSpec((None, S, QK_PAD_DIM), lambda h, i: (h, 0, 0)),
                  pl.BlockSpec((None, V_EXT_DIM, S), lambda h, i: (h, 0, 0))],
        out_specs=pl.BlockSpec((tq, V_HEAD_DIM), lambda h, i: (i, h)),
        out_shape=jax.ShapeDtypeStruct((S, H * V_HEAD_DIM), BF16),
        scratch_shapes=[pltpu.VMEM((2, tk, tq), F32)],
        compiler_params=_params("parallel", "arbitrary"),
        name="attention",
    )(qt, k, vt)


def _conv_kernel(prev_ref, cur_ref, next_ref, w_ref, b_ref, g_ref, beta_ref, o_ref,
                 buf, shifted, conv_scr, *, rb, lc):
    i = pl.program_id(0)
    ts, C = cur_ref.shape
    first = i == 0
    last = i == pl.num_programs(0) - 1
    buf[0:HALO_ROWS, :] = jnp.where(first, 0.0, prev_ref[...])
    buf[HALO_ROWS:HALO_ROWS + ts, :] = cur_ref[...]
    buf[HALO_ROWS + ts:, :] = jnp.where(last, 0.0, next_ref[...])
    off = HALO_ROWS - CONV_PAD
    n_sh = ts + SUBLANES * ((CONV_WIDTH + off - 1) // SUBLANES)

    for c0 in range(0, C, lc):
        for b in range(1, SUBLANES):
            shifted[b - 1, 0:n_sh, :] = buf[b:b + n_sh, c0:c0 + lc]

        def rows(rblk, _, c0=c0):
            r0 = pl.multiple_of(rblk * rb, rb)
            acc = jnp.broadcast_to(b_ref[:, c0:c0 + lc], (rb, lc))
            for k in range(CONV_WIDTH):
                a, b = divmod(k + off, SUBLANES)
                start = pl.multiple_of(r0 + a * SUBLANES, SUBLANES)
                if b == 0:
                    tap = buf[pl.ds(start, rb), c0:c0 + lc]
                else:
                    tap = shifted[b - 1, pl.ds(start, rb), :]
                acc = acc + w_ref[k:k + 1, c0:c0 + lc] * tap
            conv_scr[pl.ds(r0, rb), c0:c0 + lc] = acc
            return 0
        lax.fori_loop(0, ts // rb, rows, 0)

    y = conv_scr[...]
    mu = jnp.mean(y, axis=-1, keepdims=True)
    yc = y - mu
    var = jnp.mean(yc * yc, axis=-1, keepdims=True)
    z = yc * lax.rsqrt(var + NORM_EPS) * g_ref[...] + beta_ref[...]
    o_ref[...] = (z * jax.nn.sigmoid(z)).astype(BF16)


def _conv(hglu, w, b, g, beta, *, ts=256, rb=32, lc=512):
    S, C = hglu.shape
    ts = min(ts, S)
    n = S // ts
    hb = ts // HALO_ROWS
    last_halo = S // HALO_ROWS - 1
    vec = pl.BlockSpec((1, C), lambda i: (0, 0))
    return pl.pallas_call(
        functools.partial(_conv_kernel, rb=rb, lc=lc),
        grid=(n,),
        in_specs=[pl.BlockSpec((HALO_ROWS, C), lambda i: (jnp.maximum(i * hb - 1, 0), 0)),
                  pl.BlockSpec((ts, C), lambda i: (i, 0)),
                  pl.BlockSpec((HALO_ROWS, C), lambda i: (jnp.minimum((i + 1) * hb, last_halo), 0)),
                  pl.BlockSpec((CONV_WIDTH, C), lambda i: (0, 0)),
                  vec, vec, vec],
        out_specs=pl.BlockSpec((ts, C), lambda i: (i, 0)),
        out_shape=jax.ShapeDtypeStruct((S, C), BF16),
        scratch_shapes=[pltpu.VMEM((ts + 2 * HALO_ROWS, C), F32),
                        pltpu.VMEM((SUBLANES - 1, ts + 2 * HALO_ROWS, lc), F32),
                        pltpu.VMEM((ts, C), F32)],
        compiler_params=_params("parallel"),
        name="conv",
    )(hglu, hglu, hglu, w, b, g, beta)


def _merge_kernel(hn_ref, hc_ref, o_ref, wgc_ref, wgm_ref, wpw_ref, wo_ref, out_ref):
    hn = hn_ref[...]
    gc = jax.nn.sigmoid(jnp.dot(hn, wgc_ref[...], preferred_element_type=F32))
    gm = jax.nn.sigmoid(jnp.dot(hn, wgm_ref[...], preferred_element_type=F32))
    yc = jnp.dot(hc_ref[...], wpw_ref[...], preferred_element_type=F32)
    ym = jnp.dot(o_ref[...], wo_ref[...], preferred_element_type=F32)
    out_ref[...] = (gc * yc + gm * ym).astype(BF16)


def _merge(hn, hc, o, wgc, wgm, wpw, wo, *, tm=512, tn=512):
    S, D = hn.shape
    tm = min(tm, S)
    row = pl.BlockSpec((tm, D), lambda i, j: (i, 0))
    col = pl.BlockSpec((D, tn), lambda i, j: (0, j))
    return pl.pallas_call(
        _merge_kernel,
        grid=(S // tm, D // tn),
        in_specs=[row, row, row, col, col, col, col],
        out_specs=pl.BlockSpec((tm, tn), lambda i, j: (i, j)),
        out_shape=jax.ShapeDtypeStruct((S, D), BF16),
        compiler_params=_params("parallel", "arbitrary"),
        name="merge",
    )(hn, hc, o, wgc, wgm, wpw, wo)


def _outproj_kernel(x_ref, m_ref, w_ref, o_ref):
    o_ref[...] = x_ref[...] + jnp.dot(m_ref[...], w_ref[...], preferred_element_type=F32)


def _out_proj(x, merged, w, *, tm=512):
    S, D = x.shape
    tm = min(tm, S)
    row = pl.BlockSpec((tm, D), lambda i: (i, 0))
    return pl.pallas_call(
        _outproj_kernel,
        grid=(S // tm,),
        in_specs=[row, row, pl.BlockSpec((D, D), lambda i: (0, 0))],
        out_specs=row,
        out_shape=jax.ShapeDtypeStruct((S, D), F32),
        compiler_params=_params("parallel"),
        name="out_proj",
    )(x, merged, w)


def _rot_half_cols(w):
    half = QK_ROPE_DIM // 2
    return jnp.concatenate([-w[..., half:], w[..., :half]], axis=-1)


def _pad_cols(w, n):
    return jnp.pad(w, [(0, 0)] * (w.ndim - 1) + [(0, n - w.shape[-1])])


def _layer(x, pos, freq, ffn1_norm_g, ffn1_w_gate, ffn1_w_up, ffn1_w_down, mix_norm_g, w_in,
           conv_w_dw, conv_b_dw, conv_ln_g, conv_ln_b, conv_w_pw_out, mla_q_norm_g, mla_w_uq,
           mla_kv_norm_g, mla_w_ukv, mla_w_o, w_out, ffn2_norm_g, ffn2_w_gate, ffn2_w_up,
           ffn2_w_down, out_norm_g, *, final):
    D = x.shape[1]
    C = conv_w_dw.shape[1]
    row = lambda v: v.reshape(1, -1)
    bf = lambda w: w.astype(BF16)

    o_q = 2 * C
    o_kv = o_q + Q_LORA_RANK
    o_kr = o_kv + KV_LORA_RANK
    o_gc = o_kr + QK_ROPE_DIM
    o_gm = o_gc + D
    w_glu = bf(w_in[:, :o_q])
    w_kr = w_in[:, o_kr:o_gc]
    w_small = bf(jnp.concatenate(
        [w_in[:, o_q:o_kr], _pad_cols(w_kr, LANES), _pad_cols(_rot_half_cols(w_kr), LANES)], axis=1))
    w_gc = bf(w_in[:, o_gc:o_gm])
    w_gm = bf(w_in[:, o_gm:])

    wq = mla_w_uq.reshape(Q_LORA_RANK, N_HEADS, QK_NOPE_DIM + QK_ROPE_DIM)
    wq_rope = wq[..., QK_NOPE_DIM:]
    wq = jnp.concatenate([wq[..., :QK_NOPE_DIM], wq_rope, _rot_half_cols(wq_rope)], axis=-1)
    wqt = bf(wq.transpose(1, 2, 0))
    wkv = mla_w_ukv.reshape(KV_LORA_RANK, N_HEADS, QK_NOPE_DIM + V_HEAD_DIM)
    wk = bf(wkv[..., :QK_NOPE_DIM].transpose(1, 0, 2))
    wvt = bf(wkv[..., QK_NOPE_DIM:].transpose(1, 2, 0))

    x1, hn = _ffn(x, row(ffn1_norm_g), bf(ffn1_w_gate), bf(ffn1_w_up), bf(ffn1_w_down),
                  row(mix_norm_g), final=False)

    hglu = _glu_proj(hn, w_glu)
    hc = _conv(hglu, conv_w_dw, row(conv_b_dw), row(conv_ln_g), row(conv_ln_b))

    cq, ckv, kr, cst, snt = _small_proj(hn, w_small, row(mla_q_norm_g), row(mla_kv_norm_g), pos, freq)
    q_scale = (QK_NOPE_DIM + QK_ROPE_DIM) ** -0.5 * math.log2(math.e)
    qt, k, vt, qn2, kmax2 = _qkv_proj(cq, ckv, kr, cst, snt, wqt, wk, wvt, q_scale=q_scale)
    o_bounded, denom = _attention_bounded(qt, qn2, kmax2, k, vt)
    o = lax.cond(jnp.all(denom >= L_MIN), lambda: o_bounded, lambda: _attention(qt, k, vt))

    merged = _merge(hn, hc, o, w_gc, w_gm, bf(conv_w_pw_out), bf(mla_w_o))
    x2 = _out_proj(x1, merged, bf(w_out))
    (y,) = _ffn(x2, row(ffn2_norm_g), bf(ffn2_w_gate), bf(ffn2_w_up), bf(ffn2_w_down),
                row(out_norm_g), final=final)
    return y


def kernel(x, positions, ffn1_norm_g, ffn1_w_gate, ffn1_w_up, ffn1_w_down, mix_norm_g, w_in, conv_w_dw, conv_b_dw, conv_ln_g, conv_ln_b, conv_w_pw_out, mla_q_norm_g, mla_w_uq, mla_kv_norm_g, mla_w_ukv, mla_w_o, w_out, ffn2_norm_g, ffn2_w_gate, ffn2_w_up, ffn2_w_down, final_norm_g):
    B, S, D = x.shape
    depth = ffn1_norm_g.shape[0]
    assert depth == 1, "the fused final norm assumes a single layer"
    inv_freq = ROPE_THETA ** (-jnp.arange(0, QK_ROPE_DIM, 2, dtype=F32) / QK_ROPE_DIM)
    freq = _pad_cols(jnp.concatenate([inv_freq, inv_freq]), LANES).reshape(1, LANES)
    outs = []
    for b in range(B):
        y = _layer(x[b], positions[b].reshape(S, 1), freq,
                   ffn1_norm_g[0], ffn1_w_gate[0], ffn1_w_up[0], ffn1_w_down[0], mix_norm_g[0],
                   w_in[0], conv_w_dw[0], conv_b_dw[0], conv_ln_g[0], conv_ln_b[0],
                   conv_w_pw_out[0], mla_q_norm_g[0], mla_w_uq[0], mla_kv_norm_g[0], mla_w_ukv[0],
                   mla_w_o[0], w_out[0], ffn2_norm_g[0], ffn2_w_gate[0], ffn2_w_up[0],
                   ffn2_w_down[0], final_norm_g, final=True)
        outs.append(y)
    return jnp.stack(outs)
```

```python
import functools
import math

import jax
import jax.numpy as jnp
from jax import lax
from jax.experimental import pallas as pl
from jax.experimental.pallas import tpu as pltpu

N_HEADS = 16
QK_NOPE_DIM = 128
QK_ROPE_DIM = 64
V_HEAD_DIM = 128
Q_LORA_RANK = 768
KV_LORA_RANK = 512
CONV_WIDTH = 31
CONV_PAD = CONV_WIDTH // 2
ROPE_THETA = 10000.0
NORM_EPS = 1e-6

LANES = 128
SUBLANES = 8
QK_PAD_DIM = 256
QK_DIM = QK_NOPE_DIM + QK_ROPE_DIM
V_EXT_DIM = V_HEAD_DIM + 16
BOUND_MARGIN = 1.0 + 2.0 ** -6
L_MIN = 2.0 ** -80
HALO_ROWS = 16
VMEM_LIMIT = 56 * 1024 * 1024

F32 = jnp.float32
BF16 = jnp.bfloat16


def _params(*sem):
    return pltpu.CompilerParams(dimension_semantics=sem, vmem_limit_bytes=VMEM_LIMIT)


def _rms(x, g):
    return x * lax.rsqrt(jnp.mean(x * x, axis=-1, keepdims=True) + NORM_EPS) * g


def _ffn_kernel(x_ref, g_ref, wg_ref, wu_ref, wd_ref, g2_ref, *refs, final):
    if final:
        o_ref, hn_scr, acc_scr = refs
    else:
        o_ref, hn_out_ref, hn_scr, acc_scr = refs
    j = pl.program_id(1)

    @pl.when(j == 0)
    def _():
        hn_scr[...] = _rms(x_ref[...], g_ref[...]).astype(BF16)
        acc_scr[...] = jnp.zeros_like(acc_scr)

    h = hn_scr[...]
    a = jnp.dot(h, wg_ref[...], preferred_element_type=F32)
    u = jnp.dot(h, wu_ref[...], preferred_element_type=F32)
    act = (a * jax.nn.sigmoid(a) * u).astype(BF16)
    acc_scr[...] += jnp.dot(act, wd_ref[...], preferred_element_type=F32)

    @pl.when(j == pl.num_programs(1) - 1)
    def _():
        y = x_ref[...] + 0.5 * acc_scr[...]
        if final:
            o_ref[...] = _rms(y, g2_ref[...])
        else:
            o_ref[...] = y
            hn_out_ref[...] = _rms(y, g2_ref[...]).astype(BF16)


def _ffn(x, g, wg, wu, wd, g2, *, final, tm=512, tf=512):
    S, D = x.shape
    F = wg.shape[1]
    tm = min(tm, S)
    row = pl.BlockSpec((tm, D), lambda i, j: (i, 0))
    vec = pl.BlockSpec((1, D), lambda i, j: (0, 0))
    out_shape = [jax.ShapeDtypeStruct((S, D), F32)]
    out_specs = [row]
    if not final:
        out_shape.append(jax.ShapeDtypeStruct((S, D), BF16))
        out_specs.append(row)
    return pl.pallas_call(
        functools.partial(_ffn_kernel, final=final),
        grid=(S // tm, F // tf),
        in_specs=[row, vec,
                  pl.BlockSpec((D, tf), lambda i, j: (0, j)),
                  pl.BlockSpec((D, tf), lambda i, j: (0, j)),
                  pl.BlockSpec((tf, D), lambda i, j: (j, 0)),
                  vec],
        out_specs=out_specs,
        out_shape=out_shape,
        scratch_shapes=[pltpu.VMEM((tm, D), BF16), pltpu.VMEM((tm, D), F32)],
        compiler_params=_params("parallel", "arbitrary"),
        name="ffn_final" if final else "ffn_mix",
    )(x, g, wg, wu, wd, g2)


def _glu_kernel(h_ref, wa_ref, wg_ref, o_ref):
    h = h_ref[...]
    a = jnp.dot(h, wa_ref[...], preferred_element_type=F32)
    g = jnp.dot(h, wg_ref[...], preferred_element_type=F32)
    o_ref[...] = a * jax.nn.sigmoid(g)


def _glu_proj(hn, w_glu, *, tm=1024, tn=512):
    S, D = hn.shape
    C = w_glu.shape[1] // 2
    tm = min(tm, S)
    nj = C // tn
    return pl.pallas_call(
        _glu_kernel,
        grid=(S // tm, nj),
        in_specs=[pl.BlockSpec((tm, D), lambda i, j: (i, 0)),
                  pl.BlockSpec((D, tn), lambda i, j: (0, j)),
                  pl.BlockSpec((D, tn), lambda i, j: (0, j + nj))],
        out_specs=pl.BlockSpec((tm, tn), lambda i, j: (i, j)),
        out_shape=jax.ShapeDtypeStruct((S, C), F32),
        compiler_params=_params("parallel", "arbitrary"),
        name="glu_proj",
    )(hn, w_glu, w_glu)


def _small_kernel(h_ref, w_ref, gq_ref, gkv_ref, pos_col_ref, freq_row_ref, pos_row_ref, freq_col_ref,
                  cq_ref, ckv_ref, kr_ref, cst_ref, snt_ref):
    r = jnp.dot(h_ref[...], w_ref[...], preferred_element_type=F32)
    cq_ref[...] = _rms(r[:, :Q_LORA_RANK], gq_ref[...]).astype(BF16)
    c0 = Q_LORA_RANK
    ckv_ref[...] = _rms(r[:, c0:c0 + KV_LORA_RANK], gkv_ref[...]).astype(BF16)
    c1 = c0 + KV_LORA_RANK
    ang = pos_col_ref[...].astype(F32) * freq_row_ref[...]
    kr = r[:, c1:c1 + LANES] * jnp.cos(ang) + r[:, c1 + LANES:c1 + 2 * LANES] * jnp.sin(ang)
    lane = lax.broadcasted_iota(jnp.int32, kr.shape, 1)
    kr_ref[...] = jnp.where(lane == QK_ROPE_DIM, 1.0, kr).astype(BF16)
    ang_t = freq_col_ref[...] * pos_row_ref[...].astype(F32)
    cst_ref[...] = jnp.cos(ang_t)
    snt_ref[...] = jnp.sin(ang_t)


def _small_proj(hn, w_small, gq, gkv, pos, freq, *, tm=512):
    S, D = hn.shape
    tm = min(tm, S)
    row = lambda n: pl.BlockSpec((tm, n), lambda i: (i, 0))
    col = lambda n: pl.BlockSpec((n, tm), lambda i: (0, i))
    full = lambda a: pl.BlockSpec(a.shape, lambda i: (0, 0))
    pos_col, pos_row = pos.reshape(S, 1), pos.reshape(1, S)
    freq_row, freq_col = freq.reshape(1, LANES), freq.reshape(LANES, 1)[:QK_ROPE_DIM]
    return pl.pallas_call(
        _small_kernel,
        grid=(S // tm,),
        in_specs=[row(D), full(w_small), full(gq), full(gkv), row(1), full(freq_row), col(1), full(freq_col)],
        out_specs=[row(Q_LORA_RANK), row(KV_LORA_RANK), row(LANES), col(QK_ROPE_DIM), col(QK_ROPE_DIM)],
        out_shape=[jax.ShapeDtypeStruct((S, Q_LORA_RANK), BF16),
                   jax.ShapeDtypeStruct((S, KV_LORA_RANK), BF16),
                   jax.ShapeDtypeStruct((S, LANES), BF16),
                   jax.ShapeDtypeStruct((QK_ROPE_DIM, S), F32),
                   jax.ShapeDtypeStruct((QK_ROPE_DIM, S), F32)],
        compiler_params=_params("parallel"),
        name="small_proj",
    )(hn, w_small, gq, gkv, pos_col, freq_row, pos_row, freq_col)


_NT = (((1,), (1,)), ((), ()))


def _sq(x):
    x = x.astype(F32)
    return x * x


def _qkv_kernel(cq_ref, ckv_ref, kr_ref, cst_ref, snt_ref, wqt_ref, wk_ref, wvt_ref,
                qt_ref, k_ref, vt_ref, qn2_ref, kmax2_ref, *, q_scale):
    @pl.when(pl.program_id(0) == 0)
    def _():
        kmax2_ref[...] = jnp.zeros_like(kmax2_ref)

    cq = cq_ref[...]
    ckv = ckv_ref[...]
    kr = kr_ref[...]
    cst = cst_ref[...]
    snt = snt_ref[...]
    kr_n2 = jnp.sum(_sq(kr), axis=1, keepdims=True) - 1.0
    for h in range(N_HEADS):
        rt = lax.dot_general(wqt_ref[h], cq, _NT, preferred_element_type=F32)
        n0, n1, n2 = QK_NOPE_DIM, QK_DIM, QK_DIM + QK_ROPE_DIM
        q_nope = (rt[0:n0] * q_scale).astype(BF16)
        q_rope = ((rt[n0:n1] * cst + rt[n1:n2] * snt) * q_scale).astype(BF16)
        qt_ref[h, 0:n0, :] = q_nope
        qt_ref[h, n0:n1, :] = q_rope
        qt_ref[h, n1:QK_PAD_DIM, :] = jnp.zeros((QK_PAD_DIM - n1, rt.shape[1]), BF16)
        k_nope = jnp.dot(ckv, wk_ref[h], preferred_element_type=F32).astype(BF16)
        k_ref[h, :, 0:LANES] = k_nope
        k_ref[h, :, LANES:2 * LANES] = kr
        vt = lax.dot_general(wvt_ref[h], ckv, _NT, preferred_element_type=F32)
        vt_ref[h, 0:V_HEAD_DIM, :] = vt.astype(BF16)
        vt_ref[h, V_HEAD_DIM:V_EXT_DIM, :] = jnp.ones((V_EXT_DIM - V_HEAD_DIM, vt.shape[1]), BF16)
        qn2_ref[h] = jnp.sum(_sq(q_nope), axis=0, keepdims=True) + jnp.sum(_sq(q_rope), axis=0, keepdims=True)
        k_n2 = jnp.sum(_sq(k_nope), axis=1, keepdims=True) + kr_n2
        kmax2_ref[h] = jnp.maximum(kmax2_ref[h], jnp.max(k_n2, axis=0, keepdims=True))


def _qkv_proj(cq, ckv, kr, cst, snt, wqt, wk, wvt, *, q_scale, tm=256):
    S = cq.shape[0]
    tm = min(tm, S)
    row = lambda n: pl.BlockSpec((tm, n), lambda i: (i, 0))
    col = lambda n: pl.BlockSpec((n, tm), lambda i: (0, i))
    full3 = lambda a: pl.BlockSpec(a.shape, lambda i: (0, 0, 0))
    return pl.pallas_call(
        functools.partial(_qkv_kernel, q_scale=q_scale),
        grid=(S // tm,),
        in_specs=[row(Q_LORA_RANK), row(KV_LORA_RANK), row(LANES), col(QK_ROPE_DIM), col(QK_ROPE_DIM),
                  full3(wqt), full3(wk), full3(wvt)],
        out_specs=[pl.BlockSpec((N_HEADS, QK_PAD_DIM, tm), lambda i: (0, 0, i)),
                   pl.BlockSpec((N_HEADS, tm, QK_PAD_DIM), lambda i: (0, i, 0)),
                   pl.BlockSpec((N_HEADS, V_EXT_DIM, tm), lambda i: (0, 0, i)),
                   pl.BlockSpec((N_HEADS, 1, tm), lambda i: (0, 0, i)),
                   pl.BlockSpec((N_HEADS, SUBLANES, LANES), lambda i: (0, 0, 0))],
        out_shape=[jax.ShapeDtypeStruct((N_HEADS, QK_PAD_DIM, S), BF16),
                   jax.ShapeDtypeStruct((N_HEADS, S, QK_PAD_DIM), BF16),
                   jax.ShapeDtypeStruct((N_HEADS, V_EXT_DIM, S), BF16),
                   jax.ShapeDtypeStruct((N_HEADS, 1, S), F32),
                   jax.ShapeDtypeStruct((N_HEADS, SUBLANES, LANES), F32)],
        compiler_params=_params("arbitrary"),
        name="qkv_proj",
    )(cq, ckv, kr, cst, snt, wqt, wk, wvt)


def _attn_bounded_kernel(qt_ref, qn2_ref, kmax2_ref, k_ref, vt_ref, o_ref, l_ref, *, tk, unroll):
    tq = qt_ref.shape[1]
    n_kv = k_ref.shape[0] // tk
    bound = jnp.sqrt(qn2_ref[...] * kmax2_ref[0:1, 0:1]) * BOUND_MARGIN
    tile = 2 * SUBLANES
    row = lax.broadcasted_iota(jnp.int32, (tile, tq), 0)
    offset_rows = jnp.where(row == 0, -bound, 0.0).astype(BF16)
    qt = jnp.concatenate([qt_ref[0:QK_DIM, :], offset_rows, qt_ref[QK_DIM + tile:, :]], axis=0)

    def body(u, carry):
        acc, l = carry
        for j in range(unroll):
            start = pl.multiple_of((unroll * u + j) * tk, tk)
            s = jnp.dot(k_ref[pl.ds(start, tk), :], qt, preferred_element_type=F32)
            p = jnp.exp2(s)
            l = l + jnp.sum(p, axis=0, keepdims=True)
            acc = acc + jnp.dot(vt_ref[:, pl.ds(start, tk)], p.astype(BF16), preferred_element_type=F32)
        return acc, l

    zeros = (jnp.zeros((V_HEAD_DIM, tq), F32), jnp.zeros((1, tq), F32))
    acc, l = lax.fori_loop(0, n_kv // unroll, body, zeros)
    l_ref[...] = l
    o_ref[...] = (acc / l).T.astype(BF16)


def _attention_bounded(qt, qn2, kmax2, k, vt, *, tq=1024, tk=1024, unroll=8):
    H, S, _ = k.shape
    tq = min(tq, S)
    tk = min(tk, S // unroll)
    assert S % (unroll * tk) == 0 and S % tq == 0
    return pl.pallas_call(
        functools.partial(_attn_bounded_kernel, tk=tk, unroll=unroll),
        grid=(H, S // tq),
        in_specs=[pl.BlockSpec((None, QK_PAD_DIM, tq), lambda h, i: (h, 0, i)),
                  pl.BlockSpec((None, 1, tq), lambda h, i: (h, 0, i)),
                  pl.BlockSpec((None, SUBLANES, LANES), lambda h, i: (h, 0, 0)),
                  pl.BlockSpec((None, S, QK_PAD_DIM), lambda h, i: (h, 0, 0)),
                  pl.BlockSpec((None, V_HEAD_DIM, S), lambda h, i: (h, 0, 0))],
        out_specs=[pl.BlockSpec((tq, V_HEAD_DIM), lambda h, i: (i, h)),
                   pl.BlockSpec((None, 1, tq), lambda h, i: (h, 0, i))],
        out_shape=[jax.ShapeDtypeStruct((S, H * V_HEAD_DIM), BF16),
                   jax.ShapeDtypeStruct((H, 1, S), F32)],
        compiler_params=_params("parallel", "arbitrary"),
        name="attention_bounded",
    )(qt, qn2, kmax2, k, vt)


def _attn_kernel(qt_ref, k_ref, vt_ref, o_ref, s_scr, *, tk, unroll):
    qt = qt_ref[...]
    tq = qt.shape[1]
    n_kv = k_ref.shape[0] // tk

    def scores(t, slot):
        start = pl.multiple_of(t * tk, tk)
        s = jnp.dot(k_ref[pl.ds(start, tk), :], qt, preferred_element_type=F32)
        s_scr[slot] = s
        return jnp.max(s, axis=0, keepdims=True)

    def softmax_pv(slot, s_max, t, m, acc):
        m_new = jnp.maximum(m, s_max)
        alpha = jnp.exp2(m - m_new)
        p = jnp.exp2(s_scr[slot] - m_new)
        start = pl.multiple_of(t * tk, tk)
        pv = jnp.dot(vt_ref[:, pl.ds(start, tk)], p.astype(BF16), preferred_element_type=F32)
        return m_new, alpha * acc + pv

    def body(u, carry):
        m, acc, s_max = carry
        t = unroll * u
        for j in range(unroll):
            nxt = t + j + 1
            if j == unroll - 1:
                nxt = jnp.minimum(nxt, n_kv - 1)
            next_max = scores(nxt, (j + 1) % 2)
            m, acc = softmax_pv(j % 2, s_max, t + j, m, acc)
            s_max = next_max
        return m, acc, s_max

    m0 = jnp.full((1, tq), -jnp.inf, F32)
    acc0 = jnp.zeros((V_EXT_DIM, tq), F32)
    _, acc, _ = lax.fori_loop(0, n_kv // unroll, body, (m0, acc0, scores(0, 0)))
    o_ref[...] = (acc[:V_HEAD_DIM] / acc[V_HEAD_DIM:V_HEAD_DIM + 1]).T.astype(BF16)


def _attention(qt, k, vt, *, tq=512, tk=512, unroll=16):
    H, S, _ = k.shape
    tq = min(tq, S)
    tk = min(tk, S // unroll)
    assert unroll % 2 == 0 and S % (unroll * tk) == 0 and S % tq == 0
    return pl.pallas_call(
        functools.partial(_attn_kernel, tk=tk, unroll=unroll),
        grid=(H, S // tq),
        in_specs=[pl.BlockSpec((None, QK_PAD_DIM, tq), lambda h, i: (h, 0, i)),
                  pl.BlockSpec((None, S, QK_PAD_DIM), lambda h, i: (h, 0, 0)),
                  pl.BlockSpec((None, V_EXT_DIM, S), lambda h, i: (h, 0, 0))],
        out_specs=pl.BlockSpec((tq, V_HEAD_DIM), lambda h, i: (i, h)),
        out_shape=jax.ShapeDtypeStruct((S, H * V_HEAD_DIM), BF16),
        scratch_shapes=[pltpu.VMEM((2, tk, tq), F32)],
        compiler_params=_params("parallel", "arbitrary"),
        name="attention",
    )(qt, k, vt)


def _conv_kernel(prev_ref, cur_ref, next_ref, w_ref, b_ref, g_ref, beta_ref, o_ref,
                 buf, shifted, conv_scr, *, rb, lc):
    i = pl.program_id(0)
    ts, C = cur_ref.shape
    first = i == 0
    last = i == pl.num_programs(0) - 1
    buf[0:HALO_ROWS, :] = jnp.where(first, 0.0, prev_ref[...])
    buf[HALO_ROWS:HALO_ROWS + ts, :] = cur_ref[...]
    buf[HALO_ROWS + ts:, :] = jnp.where(last, 0.0, next_ref[...])
    off = HALO_ROWS - CONV_PAD
    n_sh = ts + SUBLANES * ((CONV_WIDTH + off - 1) // SUBLANES)

    for c0 in range(0, C, lc):
        for b in range(1, SUBLANES):
            shifted[b - 1, 0:n_sh, :] = buf[b:b + n_sh, c0:c0 + lc]

        def rows(rblk, _, c0=c0):
            r0 = pl.multiple_of(rblk * rb, rb)
            acc = jnp.broadcast_to(b_ref[:, c0:c0 + lc], (rb, lc))
            for k in range(CONV_WIDTH):
                a, b = divmod(k + off, SUBLANES)
                start = pl.multiple_of(r0 + a * SUBLANES, SUBLANES)
                if b == 0:
                    tap = buf[pl.ds(start, rb), c0:c0 + lc]
                else:
                    tap = shifted[b - 1, pl.ds(start, rb), :]
                acc = acc + w_ref[k:k + 1, c0:c0 + lc] * tap
            conv_scr[pl.ds(r0, rb), c0:c0 + lc] = acc
            return 0
        lax.fori_loop(0, ts // rb, rows, 0)

    y = conv_scr[...]
    mu = jnp.mean(y, axis=-1, keepdims=True)
    yc = y - mu
    var = jnp.mean(yc * yc, axis=-1, keepdims=True)
    z = yc * lax.rsqrt(var + NORM_EPS) * g_ref[...] + beta_ref[...]
    o_ref[...] = (z * jax.nn.sigmoid(z)).astype(BF16)


def _conv(hglu, w, b, g, beta, *, ts=256, rb=128, lc=128):
    S, C = hglu.shape
    ts = min(ts, S)
    n = S // ts
    hb = ts // HALO_ROWS
    last_halo = S // HALO_ROWS - 1
    vec = pl.BlockSpec((1, C), lambda i: (0, 0))
    return pl.pallas_call(
        functools.partial(_conv_kernel, rb=rb, lc=lc),
        grid=(n,),
        in_specs=[pl.BlockSpec((HALO_ROWS, C), lambda i: (jnp.maximum(i * hb - 1, 0), 0)),
                  pl.BlockSpec((ts, C), lambda i: (i, 0)),
                  pl.BlockSpec((HALO_ROWS, C), lambda i: (jnp.minimum((i + 1) * hb, last_halo), 0)),
                  pl.BlockSpec((CONV_WIDTH, C), lambda i: (0, 0)),
                  vec, vec, vec],
        out_specs=pl.BlockSpec((ts, C), lambda i: (i, 0)),
        out_shape=jax.ShapeDtypeStruct((S, C), BF16),
        scratch_shapes=[pltpu.VMEM((ts + 2 * HALO_ROWS, C), F32),
                        pltpu.VMEM((SUBLANES - 1, ts + 2 * HALO_ROWS, lc), F32),
                        pltpu.VMEM((ts, C), F32)],
        compiler_params=_params("parallel"),
        name="conv",
    )(hglu, hglu, hglu, w, b, g, beta)


def _merge_kernel(hn_ref, hc_ref, o_ref, wgc_ref, wgm_ref, wpw_ref, wo_ref, out_ref):
    hn = hn_ref[...]
    gc = jax.nn.sigmoid(jnp.dot(hn, wgc_ref[...], preferred_element_type=F32))
    gm = jax.nn.sigmoid(jnp.dot(hn, wgm_ref[...], preferred_element_type=F32))
    yc = jnp.dot(hc_ref[...], wpw_ref[...], preferred_element_type=F32)
    ym = jnp.dot(o_ref[...], wo_ref[...], preferred_element_type=F32)
    out_ref[...] = (gc * yc + gm * ym).astype(BF16)


def _merge(hn, hc, o, wgc, wgm, wpw, wo, *, tm=512, tn=512):
    S, D = hn.shape
    tm = min(tm, S)
    row = pl.BlockSpec((tm, D), lambda i, j: (i, 0))
    col = pl.BlockSpec((D, tn), lambda i, j: (0, j))
    return pl.pallas_call(
        _merge_kernel,
        grid=(S // tm, D // tn),
        in_specs=[row, row, row, col, col, col, col],
        out_specs=pl.BlockSpec((tm, tn), lambda i, j: (i, j)),
        out_shape=jax.ShapeDtypeStruct((S, D), BF16),
        compiler_params=_params("parallel", "arbitrary"),
        name="merge",
    )(hn, hc, o, wgc, wgm, wpw, wo)


def _outproj_kernel(x_ref, m_ref, w_ref, o_ref):
    o_ref[...] = x_ref[...] + jnp.dot(m_ref[...], w_ref[...], preferred_element_type=F32)


def _out_proj(x, merged, w, *, tm=512):
    S, D = x.shape
    tm = min(tm, S)
    row = pl.BlockSpec((tm, D), lambda i: (i, 0))
    return pl.pallas_call(
        _outproj_kernel,
        grid=(S // tm,),
        in_specs=[row, row, pl.BlockSpec((D, D), lambda i: (0, 0))],
        out_specs=row,
        out_shape=jax.ShapeDtypeStruct((S, D), F32),
        compiler_params=_params("parallel"),
        name="out_proj",
    )(x, merged, w)


def _rot_half_cols(w):
    half = QK_ROPE_DIM // 2
    return jnp.concatenate([-w[..., half:], w[..., :half]], axis=-1)


def _pad_cols(w, n):
    return jnp.pad(w, [(0, 0)] * (w.ndim - 1) + [(0, n - w.shape[-1])])


def _layer(x, pos, freq, ffn1_norm_g, ffn1_w_gate, ffn1_w_up, ffn1_w_down, mix_norm_g, w_in,
           conv_w_dw, conv_b_dw, conv_ln_g, conv_ln_b, conv_w_pw_out, mla_q_norm_g, mla_w_uq,
           mla_kv_norm_g, mla_w_ukv, mla_w_o, w_out, ffn2_norm_g, ffn2_w_gate, ffn2_w_up,
           ffn2_w_down, out_norm_g, *, final):
    D = x.shape[1]
    C = conv_w_dw.shape[1]
    row = lambda v: v.reshape(1, -1)
    bf = lambda w: w.astype(BF16)

    o_q = 2 * C
    o_kv = o_q + Q_LORA_RANK
    o_kr = o_kv + KV_LORA_RANK
    o_gc = o_kr + QK_ROPE_DIM
    o_gm = o_gc + D
    w_in = lax.optimization_barrier(bf(w_in))
    w_glu = w_in[:, :o_q]
    w_kr = w_in[:, o_kr:o_gc]
    w_small = jnp.concatenate(
        [w_in[:, o_q:o_kr], _pad_cols(w_kr, LANES), _pad_cols(_rot_half_cols(w_kr), LANES)], axis=1)
    w_gc = w_in[:, o_gc:o_gm]
    w_gm = w_in[:, o_gm:]

    wq = lax.optimization_barrier(bf(mla_w_uq)).reshape(Q_LORA_RANK, N_HEADS, QK_NOPE_DIM + QK_ROPE_DIM)
    wq_rope = wq[..., QK_NOPE_DIM:]
    wq = jnp.concatenate([wq[..., :QK_NOPE_DIM], wq_rope, _rot_half_cols(wq_rope)], axis=-1)
    wqt = wq.transpose(1, 2, 0)
    wkv = lax.optimization_barrier(bf(mla_w_ukv)).reshape(KV_LORA_RANK, N_HEADS, QK_NOPE_DIM + V_HEAD_DIM)
    wk = wkv[..., :QK_NOPE_DIM].transpose(1, 0, 2)
    wvt = wkv[..., QK_NOPE_DIM:].transpose(1, 2, 0)

    x1, hn = _ffn(x, row(ffn1_norm_g), bf(ffn1_w_gate), bf(ffn1_w_up), bf(ffn1_w_down),
                  row(mix_norm_g), final=False)

    hglu = _glu_proj(hn, w_glu)
    hc = _conv(hglu, conv_w_dw, row(conv_b_dw), row(conv_ln_g), row(conv_ln_b))

    cq, ckv, kr, cst, snt = _small_proj(hn, w_small, row(mla_q_norm_g), row(mla_kv_norm_g), pos, freq)
    q_scale = (QK_NOPE_DIM + QK_ROPE_DIM) ** -0.5 * math.log2(math.e)
    qt, k, vt, qn2, kmax2 = _qkv_proj(cq, ckv, kr, cst, snt, wqt, wk, wvt, q_scale=q_scale)
    o_bounded, denom = _attention_bounded(qt, qn2, kmax2, k, vt)
    o = lax.cond(jnp.all(denom >= L_MIN), lambda: o_bounded, lambda: _attention(qt, k, vt))

    merged = _merge(hn, hc, o, w_gc, w_gm, bf(conv_w_pw_out), bf(mla_w_o))
    x2 = _out_proj(x1, merged, bf(w_out))
    (y,) = _ffn(x2, row(ffn2_norm_g), bf(ffn2_w_gate), bf(ffn2_w_up), bf(ffn2_w_down),
                row(out_norm_g), final=final)
    return y


def kernel(x, positions, ffn1_norm_g, ffn1_w_gate, ffn1_w_up, ffn1_w_down, mix_norm_g, w_in, conv_w_dw, conv_b_dw, conv_ln_g, conv_ln_b, conv_w_pw_out, mla_q_norm_g, mla_w_uq, mla_kv_norm_g, mla_w_ukv, mla_w_o, w_out, ffn2_norm_g, ffn2_w_gate, ffn2_w_up, ffn2_w_down, final_norm_g):
    B, S, D = x.shape
    depth = ffn1_norm_g.shape[0]
    assert depth == 1, "the fused final norm assumes a single layer"
    inv_freq = ROPE_THETA ** (-jnp.arange(0, QK_ROPE_DIM, 2, dtype=F32) / QK_ROPE_DIM)
    freq = _pad_cols(jnp.concatenate([inv_freq, inv_freq]), LANES).reshape(1, LANES)
    outs = []
    for b in range(B):
        y = _layer(x[b], positions[b].reshape(S, 1), freq,
                   ffn1_norm_g[0], ffn1_w_gate[0], ffn1_w_up[0], ffn1_w_down[0], mix_norm_g[0],
                   w_in[0], conv_w_dw[0], conv_b_dw[0], conv_ln_g[0], conv_ln_b[0],
                   conv_w_pw_out[0], mla_q_norm_g[0], mla_w_uq[0], mla_kv_norm_g[0], mla_w_ukv[0],
                   mla_w_o[0], w_out[0], ffn2_norm_g[0], ffn2_w_gate[0], ffn2_w_up[0],
                   ffn2_w_down[0], final_norm_g, final=True)
        outs.append(y)
    return jnp.stack(outs)
```

```python
import functools
import math

import jax
import jax.numpy as jnp
from jax import lax
from jax.experimental import pallas as pl
from jax.experimental.pallas import tpu as pltpu

N_HEADS = 16
QK_NOPE_DIM = 128
QK_ROPE_DIM = 64
V_HEAD_DIM = 128
Q_LORA_RANK = 768
KV_LORA_RANK = 512
CONV_WIDTH = 31
CONV_PAD = CONV_WIDTH // 2
ROPE_THETA = 10000.0
NORM_EPS = 1e-6

LANES = 128
SUBLANES = 8
QK_PAD_DIM = 256
QK_DIM = QK_NOPE_DIM + QK_ROPE_DIM
V_EXT_DIM = V_HEAD_DIM + 16
BOUND_MARGIN = 1.0 + 2.0 ** -6
L_MIN = 2.0 ** -80
HALO_ROWS = 16
VMEM_LIMIT = 56 * 1024 * 1024

F32 = jnp.float32
BF16 = jnp.bfloat16


def _params(*sem):
    return pltpu.CompilerParams(dimension_semantics=sem, vmem_limit_bytes=VMEM_LIMIT)


def _rms(x, g):
    return x * lax.rsqrt(jnp.mean(x * x, axis=-1, keepdims=True) + NORM_EPS) * g


def _ffn_kernel(x_ref, g_ref, wg_ref, wu_ref, wd_ref, g2_ref, *refs, final):
    if final:
        o_ref, hn_scr, acc_scr = refs
    else:
        o_ref, hn_out_ref, hn_scr, acc_scr = refs
    j = pl.program_id(1)

    @pl.when(j == 0)
    def _():
        hn_scr[...] = _rms(x_ref[...], g_ref[...]).astype(BF16)
        acc_scr[...] = jnp.zeros_like(acc_scr)

    h = hn_scr[...]
    a = jnp.dot(h, wg_ref[...], preferred_element_type=F32)
    u = jnp.dot(h, wu_ref[...], preferred_element_type=F32)
    act = (a * jax.nn.sigmoid(a) * u).astype(BF16)
    acc_scr[...] += jnp.dot(act, wd_ref[...], preferred_element_type=F32)

    @pl.when(j == pl.num_programs(1) - 1)
    def _():
        y = x_ref[...] + 0.5 * acc_scr[...]
        if final:
            o_ref[...] = _rms(y, g2_ref[...])
        else:
            o_ref[...] = y
            hn_out_ref[...] = _rms(y, g2_ref[...]).astype(BF16)


def _ffn(x, g, wg, wu, wd, g2, *, final, tm=512, tf=512):
    S, D = x.shape
    F = wg.shape[1]
    tm = min(tm, S)
    row = pl.BlockSpec((tm, D), lambda i, j: (i, 0))
    vec = pl.BlockSpec((1, D), lambda i, j: (0, 0))
    out_shape = [jax.ShapeDtypeStruct((S, D), F32)]
    out_specs = [row]
    if not final:
        out_shape.append(jax.ShapeDtypeStruct((S, D), BF16))
        out_specs.append(row)
    return pl.pallas_call(
        functools.partial(_ffn_kernel, final=final),
        grid=(S // tm, F // tf),
        in_specs=[row, vec,
                  pl.BlockSpec((D, tf), lambda i, j: (0, j)),
                  pl.BlockSpec((D, tf), lambda i, j: (0, j)),
                  pl.BlockSpec((tf, D), lambda i, j: (j, 0)),
                  vec],
        out_specs=out_specs,
        out_shape=out_shape,
        scratch_shapes=[pltpu.VMEM((tm, D), BF16), pltpu.VMEM((tm, D), F32)],
        compiler_params=_params("parallel", "arbitrary"),
        name="ffn_final" if final else "ffn_mix",
    )(x, g, wg, wu, wd, g2)


def _glu_kernel(h_ref, wa_ref, wg_ref, o_ref):
    h = h_ref[...]
    a = jnp.dot(h, wa_ref[...], preferred_element_type=F32)
    g = jnp.dot(h, wg_ref[...], preferred_element_type=F32)
    o_ref[...] = a * jax.nn.sigmoid(g)


def _glu_proj(hn, w_glu, *, tm=1024, tn=512):
    S, D = hn.shape
    C = w_glu.shape[1] // 2
    tm = min(tm, S)
    nj = C // tn
    return pl.pallas_call(
        _glu_kernel,
        grid=(S // tm, nj),
        in_specs=[pl.BlockSpec((tm, D), lambda i, j: (i, 0)),
                  pl.BlockSpec((D, tn), lambda i, j: (0, j)),
                  pl.BlockSpec((D, tn), lambda i, j: (0, j + nj))],
        out_specs=pl.BlockSpec((tm, tn), lambda i, j: (i, j)),
        out_shape=jax.ShapeDtypeStruct((S, C), F32),
        compiler_params=_params("parallel", "arbitrary"),
        name="glu_proj",
    )(hn, w_glu, w_glu)


def _small_kernel(h_ref, w_ref, gq_ref, gkv_ref, pos_ref, freq_ref,
                  cq_ref, ckv_ref, kr_ref, cst_ref, snt_ref):
    r = jnp.dot(h_ref[...], w_ref[...], preferred_element_type=F32)
    cq_ref[...] = _rms(r[:, :Q_LORA_RANK], gq_ref[...]).astype(BF16)
    c0 = Q_LORA_RANK
    ckv_ref[...] = _rms(r[:, c0:c0 + KV_LORA_RANK], gkv_ref[...]).astype(BF16)
    c1 = c0 + KV_LORA_RANK
    ang = freq_ref[...] * pos_ref[...].astype(F32)
    cos, sin = jnp.cos(ang), jnp.sin(ang)
    cst = jnp.concatenate([cos, cos], axis=0)
    snt = jnp.concatenate([sin, sin], axis=0)
    cst_ref[...] = cst
    snt_ref[...] = snt
    pad = LANES - QK_ROPE_DIM
    cs = jnp.concatenate([cst, jnp.ones((pad, cst.shape[1]), F32)], axis=0).T
    sn = jnp.concatenate([snt, jnp.zeros((pad, snt.shape[1]), F32)], axis=0).T
    kr = r[:, c1:c1 + LANES] * cs + r[:, c1 + LANES:c1 + 2 * LANES] * sn
    lane = lax.broadcasted_iota(jnp.int32, kr.shape, 1)
    kr_ref[...] = jnp.where(lane == QK_ROPE_DIM, 1.0, kr).astype(BF16)


def _small_proj(hn, w_small, gq, gkv, pos, freq, *, tm=512):
    S, D = hn.shape
    tm = min(tm, S)
    row = lambda n: pl.BlockSpec((tm, n), lambda i: (i, 0))
    col = lambda n: pl.BlockSpec((n, tm), lambda i: (0, i))
    full = lambda a: pl.BlockSpec(a.shape, lambda i: (0, 0))
    return pl.pallas_call(
        _small_kernel,
        grid=(S // tm,),
        in_specs=[row(D), full(w_small), full(gq), full(gkv), col(1), full(freq)],
        out_specs=[row(Q_LORA_RANK), row(KV_LORA_RANK), row(LANES), col(QK_ROPE_DIM), col(QK_ROPE_DIM)],
        out_shape=[jax.ShapeDtypeStruct((S, Q_LORA_RANK), BF16),
                   jax.ShapeDtypeStruct((S, KV_LORA_RANK), BF16),
                   jax.ShapeDtypeStruct((S, LANES), BF16),
                   jax.ShapeDtypeStruct((QK_ROPE_DIM, S), F32),
                   jax.ShapeDtypeStruct((QK_ROPE_DIM, S), F32)],
        compiler_params=_params("parallel"),
        name="small_proj",
    )(hn, w_small, gq, gkv, pos, freq)


_NT = (((1,), (1,)), ((), ()))


def _sq(x):
    x = x.astype(F32)
    return x * x


def _qkv_kernel(cq_ref, ckv_ref, kr_ref, cst_ref, snt_ref, wqt_ref, wk_ref, wvt_ref,
                qt_ref, k_ref, vt_ref, qn2_ref, kmax2_ref, *, q_scale):
    @pl.when(pl.program_id(0) == 0)
    def _():
        kmax2_ref[...] = jnp.zeros_like(kmax2_ref)

    cq = cq_ref[...]
    ckv = ckv_ref[...]
    kr = kr_ref[...]
    cst = cst_ref[...]
    snt = snt_ref[...]
    kr_n2 = jnp.sum(_sq(kr), axis=1, keepdims=True) - 1.0
    for h in range(N_HEADS):
        rt = lax.dot_general(wqt_ref[h], cq, _NT, preferred_element_type=F32)
        n0, n1, n2 = QK_NOPE_DIM, QK_DIM, QK_DIM + QK_ROPE_DIM
        q_nope = (rt[0:n0] * q_scale).astype(BF16)
        q_rope = ((rt[n0:n1] * cst + rt[n1:n2] * snt) * q_scale).astype(BF16)
        qt_ref[h, 0:n0, :] = q_nope
        qt_ref[h, n0:n1, :] = q_rope
        qt_ref[h, n1:QK_PAD_DIM, :] = jnp.zeros((QK_PAD_DIM - n1, rt.shape[1]), BF16)
        k_nope = jnp.dot(ckv, wk_ref[h], preferred_element_type=F32).astype(BF16)
        k_ref[h, :, 0:LANES] = k_nope
        k_ref[h, :, LANES:2 * LANES] = kr
        vt = lax.dot_general(wvt_ref[h], ckv, _NT, preferred_element_type=F32)
        vt_ref[h, 0:V_HEAD_DIM, :] = vt.astype(BF16)
        vt_ref[h, V_HEAD_DIM:V_EXT_DIM, :] = jnp.ones((V_EXT_DIM - V_HEAD_DIM, vt.shape[1]), BF16)
        qn2_ref[h] = jnp.sum(_sq(q_nope), axis=0, keepdims=True) + jnp.sum(_sq(q_rope), axis=0, keepdims=True)
        k_n2 = jnp.sum(_sq(k_nope), axis=1, keepdims=True) + kr_n2
        kmax2_ref[h] = jnp.maximum(kmax2_ref[h], jnp.max(k_n2, axis=0, keepdims=True))


def _qkv_proj(cq, ckv, kr, cst, snt, wqt, wk, wvt, *, q_scale, tm=256):
    S = cq.shape[0]
    tm = min(tm, S)
    row = lambda n: pl.BlockSpec((tm, n), lambda i: (i, 0))
    col = lambda n: pl.BlockSpec((n, tm), lambda i: (0, i))
    full3 = lambda a: pl.BlockSpec(a.shape, lambda i: (0, 0, 0))
    return pl.pallas_call(
        functools.partial(_qkv_kernel, q_scale=q_scale),
        grid=(S // tm,),
        in_specs=[row(Q_LORA_RANK), row(KV_LORA_RANK), row(LANES), col(QK_ROPE_DIM), col(QK_ROPE_DIM),
                  full3(wqt), full3(wk), full3(wvt)],
        out_specs=[pl.BlockSpec((N_HEADS, QK_PAD_DIM, tm), lambda i: (0, 0, i)),
                   pl.BlockSpec((N_HEADS, tm, QK_PAD_DIM), lambda i: (0, i, 0)),
                   pl.BlockSpec((N_HEADS, V_EXT_DIM, tm), lambda i: (0, 0, i)),
                   pl.BlockSpec((N_HEADS, 1, tm), lambda i: (0, 0, i)),
                   pl.BlockSpec((N_HEADS, SUBLANES, LANES), lambda i: (0, 0, 0))],
        out_shape=[jax.ShapeDtypeStruct((N_HEADS, QK_PAD_DIM, S), BF16),
                   jax.ShapeDtypeStruct((N_HEADS, S, QK_PAD_DIM), BF16),
                   jax.ShapeDtypeStruct((N_HEADS, V_EXT_DIM, S), BF16),
                   jax.ShapeDtypeStruct((N_HEADS, 1, S), F32),
                   jax.ShapeDtypeStruct((N_HEADS, SUBLANES, LANES), F32)],
        compiler_params=_params("arbitrary"),
        name="qkv_proj",
    )(cq, ckv, kr, cst, snt, wqt, wk, wvt)


def _attn_bounded_kernel(qt_ref, qn2_ref, kmax2_ref, k_ref, vt_ref, o_ref, l_ref, *, tk, unroll):
    tq = qt_ref.shape[1]
    n_kv = k_ref.shape[0] // tk
    bound = jnp.sqrt(qn2_ref[...] * kmax2_ref[0:1, 0:1]) * BOUND_MARGIN
    tile = 2 * SUBLANES
    row = lax.broadcasted_iota(jnp.int32, (tile, tq), 0)
    offset_rows = jnp.where(row == 0, -bound, 0.0).astype(BF16)
    qt = jnp.concatenate([qt_ref[0:QK_DIM, :], offset_rows, qt_ref[QK_DIM + tile:, :]], axis=0)

    def body(u, carry):
        acc, l = carry
        for j in range(unroll):
            start = pl.multiple_of((unroll * u + j) * tk, tk)
            s = jnp.dot(k_ref[pl.ds(start, tk), :], qt, preferred_element_type=F32)
            p = jnp.exp2(s)
            l = l + jnp.sum(p, axis=0, keepdims=True)
            acc = acc + jnp.dot(vt_ref[:, pl.ds(start, tk)], p.astype(BF16), preferred_element_type=F32)
        return acc, l

    zeros = (jnp.zeros((V_HEAD_DIM, tq), F32), jnp.zeros((1, tq), F32))
    acc, l = lax.fori_loop(0, n_kv // unroll, body, zeros)
    l_ref[...] = l
    o_ref[...] = (acc / l).T.astype(BF16)


def _attention_bounded(qt, qn2, kmax2, k, vt, *, tq=1024, tk=2048, unroll=8):
    H, S, _ = k.shape
    tq = min(tq, S)
    tk = min(tk, S // unroll)
    assert S % (unroll * tk) == 0 and S % tq == 0
    return pl.pallas_call(
        functools.partial(_attn_bounded_kernel, tk=tk, unroll=unroll),
        grid=(H, S // tq),
        in_specs=[pl.BlockSpec((None, QK_PAD_DIM, tq), lambda h, i: (h, 0, i)),
                  pl.BlockSpec((None, 1, tq), lambda h, i: (h, 0, i)),
                  pl.BlockSpec((None, SUBLANES, LANES), lambda h, i: (h, 0, 0)),
                  pl.BlockSpec((None, S, QK_PAD_DIM), lambda h, i: (h, 0, 0)),
                  pl.BlockSpec((None, V_HEAD_DIM, S), lambda h, i: (h, 0, 0))],
        out_specs=[pl.BlockSpec((tq, V_HEAD_DIM), lambda h, i: (i, h)),
                   pl.BlockSpec((None, 1, tq), lambda h, i: (h, 0, i))],
        out_shape=[jax.ShapeDtypeStruct((S, H * V_HEAD_DIM), BF16),
                   jax.ShapeDtypeStruct((H, 1, S), F32)],
        compiler_params=_params("parallel", "arbitrary"),
        name="attention_bounded",
    )(qt, qn2, kmax2, k, vt)


def _attn_kernel(qt_ref, k_ref, vt_ref, o_ref, s_scr, *, tk, unroll):
    qt = qt_ref[...]
    tq = qt.shape[1]
    n_kv = k_ref.shape[0] // tk

    def scores(t, slot):
        start = pl.multiple_of(t * tk, tk)
        s = jnp.dot(k_ref[pl.ds(start, tk), :], qt, preferred_element_type=F32)
        s_scr[slot] = s
        return jnp.max(s, axis=0, keepdims=True)

    def softmax_pv(slot, s_max, t, m, acc):
        m_new = jnp.maximum(m, s_max)
        alpha = jnp.exp2(m - m_new)
        p = jnp.exp2(s_scr[slot] - m_new)
        start = pl.multiple_of(t * tk, tk)
        pv = jnp.dot(vt_ref[:, pl.ds(start, tk)], p.astype(BF16), preferred_element_type=F32)
        return m_new, alpha * acc + pv

    def body(u, carry):
        m, acc, s_max = carry
        t = unroll * u
        for j in range(unroll):
            nxt = t + j + 1
            if j == unroll - 1:
                nxt = jnp.minimum(nxt, n_kv - 1)
            next_max = scores(nxt, (j + 1) % 2)
            m, acc = softmax_pv(j % 2, s_max, t + j, m, acc)
            s_max = next_max
        return m, acc, s_max

    m0 = jnp.full((1, tq), -jnp.inf, F32)
    acc0 = jnp.zeros((V_EXT_DIM, tq), F32)
    _, acc, _ = lax.fori_loop(0, n_kv // unroll, body, (m0, acc0, scores(0, 0)))
    o_ref[...] = (acc[:V_HEAD_DIM] / acc[V_HEAD_DIM:V_HEAD_DIM + 1]).T.astype(BF16)


def _attention(qt, k, vt, *, tq=512, tk=512, unroll=16):
    H, S, _ = k.shape
    tq = min(tq, S)
    tk = min(tk, S // unroll)
    assert unroll % 2 == 0 and S % (unroll * tk) == 0 and S % tq == 0
    return pl.pallas_call(
        functools.partial(_attn_kernel, tk=tk, unroll=unroll),
        grid=(H, S // tq),
        in_specs=[pl.BlockSpec((None, QK_PAD_DIM, tq), lambda h, i: (h, 0, i)),
                  pl.BlockSpec((None, S, QK_PAD_DIM), lambda h, i: (h, 0, 0)),
                  pl.BlockSpec((None, V_EXT_DIM, S), lambda h, i: (h, 0, 0))],
        out_specs=pl.BlockSpec((tq, V_HEAD_DIM), lambda h, i: (i, h)),
        out_shape=jax.ShapeDtypeStruct((S, H * V_HEAD_DIM), BF16),
        scratch_shapes=[pltpu.VMEM((2, tk, tq), F32)],
        compiler_params=_params("parallel", "arbitrary"),
        name="attention",
    )(qt, k, vt)


def _conv_kernel(prev_ref, cur_ref, next_ref, w_ref, b_ref, g_ref, beta_ref, o_ref,
                 buf, shifted, conv_scr, *, rb, lc):
    i = pl.program_id(0)
    ts, C = cur_ref.shape
    first = i == 0
    last = i == pl.num_programs(0) - 1
    buf[0:HALO_ROWS, :] = jnp.where(first, 0.0, prev_ref[...])
    buf[HALO_ROWS:HALO_ROWS + ts, :] = cur_ref[...]
    buf[HALO_ROWS + ts:, :] = jnp.where(last, 0.0, next_ref[...])
    off = HALO_ROWS - CONV_PAD
    n_sh = ts + SUBLANES * ((CONV_WIDTH + off - 1) // SUBLANES)

    for c0 in range(0, C, lc):
        for b in range(1, SUBLANES):
            shifted[b - 1, 0:n_sh, :] = buf[b:b + n_sh, c0:c0 + lc]

        def rows(rblk, _, c0=c0):
            r0 = pl.multiple_of(rblk * rb, rb)
            acc = jnp.broadcast_to(b_ref[:, c0:c0 + lc], (rb, lc))
            for k in range(CONV_WIDTH):
                a, b = divmod(k + off, SUBLANES)
                start = pl.multiple_of(r0 + a * SUBLANES, SUBLANES)
                if b == 0:
                    tap = buf[pl.ds(start, rb), c0:c0 + lc]
                else:
                    tap = shifted[b - 1, pl.ds(start, rb), :]
                acc = acc + w_ref[k:k + 1, c0:c0 + lc] * tap
            conv_scr[pl.ds(r0, rb), c0:c0 + lc] = acc
            return 0
        lax.fori_loop(0, ts // rb, rows, 0)

    y = conv_scr[...]
    mu = jnp.mean(y, axis=-1, keepdims=True)
    yc = y - mu
    var = jnp.mean(yc * yc, axis=-1, keepdims=True)
    z = yc * lax.rsqrt(var + NORM_EPS) * g_ref[...] + beta_ref[...]
    o_ref[...] = (z * jax.nn.sigmoid(z)).astype(BF16)


def _conv(hglu, w, b, g, beta, *, ts=256, rb=128, lc=128):
    S, C = hglu.shape
    ts = min(ts, S)
    n = S // ts
    hb = ts // HALO_ROWS
    last_halo = S // HALO_ROWS - 1
    vec = pl.BlockSpec((1, C), lambda i: (0, 0))
    return pl.pallas_call(
        functools.partial(_conv_kernel, rb=rb, lc=lc),
        grid=(n,),
        in_specs=[pl.BlockSpec((HALO_ROWS, C), lambda i: (jnp.maximum(i * hb - 1, 0), 0)),
                  pl.BlockSpec((ts, C), lambda i: (i, 0)),
                  pl.BlockSpec((HALO_ROWS, C), lambda i: (jnp.minimum((i + 1) * hb, last_halo), 0)),
                  pl.BlockSpec((CONV_WIDTH, C), lambda i: (0, 0)),
                  vec, vec, vec],
        out_specs=pl.BlockSpec((ts, C), lambda i: (i, 0)),
        out_shape=jax.ShapeDtypeStruct((S, C), BF16),
        scratch_shapes=[pltpu.VMEM((ts + 2 * HALO_ROWS, C), F32),
                        pltpu.VMEM((SUBLANES - 1, ts + 2 * HALO_ROWS, lc), F32),
                        pltpu.VMEM((ts, C), F32)],
        compiler_params=_params("parallel"),
        name="conv",
    )(hglu, hglu, hglu, w, b, g, beta)


def _merge_kernel(hn_ref, hc_ref, o_ref, wgc_ref, wgm_ref, wpw_ref, wo_ref, out_ref):
    hn = hn_ref[...]
    gc = jax.nn.sigmoid(jnp.dot(hn, wgc_ref[...], preferred_element_type=F32))
    gm = jax.nn.sigmoid(jnp.dot(hn, wgm_ref[...], preferred_element_type=F32))
    yc = jnp.dot(hc_ref[...], wpw_ref[...], preferred_element_type=F32)
    ym = jnp.dot(o_ref[...], wo_ref[...], preferred_element_type=F32)
    out_ref[...] = (gc * yc + gm * ym).astype(BF16)


def _merge(hn, hc, o, wgc, wgm, wpw, wo, *, tm=512, tn=512):
    S, D = hn.shape
    tm = min(tm, S)
    row = pl.BlockSpec((tm, D), lambda i, j: (i, 0))
    col = pl.BlockSpec((D, tn), lambda i, j: (0, j))
    return pl.pallas_call(
        _merge_kernel,
        grid=(S // tm, D // tn),
        in_specs=[row, row, row, col, col, col, col],
        out_specs=pl.BlockSpec((tm, tn), lambda i, j: (i, j)),
        out_shape=jax.ShapeDtypeStruct((S, D), BF16),
        compiler_params=_params("parallel", "arbitrary"),
        name="merge",
    )(hn, hc, o, wgc, wgm, wpw, wo)


def _outproj_kernel(x_ref, m_ref, w_ref, o_ref):
    o_ref[...] = x_ref[...] + jnp.dot(m_ref[...], w_ref[...], preferred_element_type=F32)


def _out_proj(x, merged, w, *, tm=512):
    S, D = x.shape
    tm = min(tm, S)
    row = pl.BlockSpec((tm, D), lambda i: (i, 0))
    return pl.pallas_call(
        _outproj_kernel,
        grid=(S // tm,),
        in_specs=[row, row, pl.BlockSpec((D, D), lambda i: (0, 0))],
        out_specs=row,
        out_shape=jax.ShapeDtypeStruct((S, D), F32),
        compiler_params=_params("parallel"),
        name="out_proj",
    )(x, merged, w)


def _rot_half_cols(w):
    half = QK_ROPE_DIM // 2
    return jnp.concatenate([-w[..., half:], w[..., :half]], axis=-1)


def _pad_cols(w, n):
    return jnp.pad(w, [(0, 0)] * (w.ndim - 1) + [(0, n - w.shape[-1])])


def _layer(x, pos, freq, ffn1_norm_g, ffn1_w_gate, ffn1_w_up, ffn1_w_down, mix_norm_g, w_in,
           conv_w_dw, conv_b_dw, conv_ln_g, conv_ln_b, conv_w_pw_out, mla_q_norm_g, mla_w_uq,
           mla_kv_norm_g, mla_w_ukv, mla_w_o, w_out, ffn2_norm_g, ffn2_w_gate, ffn2_w_up,
           ffn2_w_down, out_norm_g, *, final):
    D = x.shape[1]
    C = conv_w_dw.shape[1]
    row = lambda v: v.reshape(1, -1)
    bf = lambda w: w.astype(BF16)

    o_q = 2 * C
    o_kv = o_q + Q_LORA_RANK
    o_kr = o_kv + KV_LORA_RANK
    o_gc = o_kr + QK_ROPE_DIM
    o_gm = o_gc + D
    w_in = lax.optimization_barrier(bf(w_in))
    w_glu = w_in[:, :o_q]
    w_kr = w_in[:, o_kr:o_gc]
    w_small = jnp.concatenate(
        [w_in[:, o_q:o_kr], _pad_cols(w_kr, LANES), _pad_cols(_rot_half_cols(w_kr), LANES)], axis=1)
    w_gc = w_in[:, o_gc:o_gm]
    w_gm = w_in[:, o_gm:]

    wq = lax.optimization_barrier(bf(mla_w_uq)).reshape(Q_LORA_RANK, N_HEADS, QK_NOPE_DIM + QK_ROPE_DIM)
    wq_rope = wq[..., QK_NOPE_DIM:]
    wq = jnp.concatenate([wq[..., :QK_NOPE_DIM], wq_rope, _rot_half_cols(wq_rope)], axis=-1)
    wqt = wq.transpose(1, 2, 0)
    wkv = lax.optimization_barrier(bf(mla_w_ukv)).reshape(KV_LORA_RANK, N_HEADS, QK_NOPE_DIM + V_HEAD_DIM)
    wk = wkv[..., :QK_NOPE_DIM].transpose(1, 0, 2)
    wvt = wkv[..., QK_NOPE_DIM:].transpose(1, 2, 0)

    x1, hn = _ffn(x, row(ffn1_norm_g), bf(ffn1_w_gate), bf(ffn1_w_up), bf(ffn1_w_down),
                  row(mix_norm_g), final=False)

    hglu = _glu_proj(hn, w_glu)
    hc = _conv(hglu, conv_w_dw, row(conv_b_dw), row(conv_ln_g), row(conv_ln_b))

    cq, ckv, kr, cst, snt = _small_proj(hn, w_small, row(mla_q_norm_g), row(mla_kv_norm_g), pos, freq)
    q_scale = (QK_NOPE_DIM + QK_ROPE_DIM) ** -0.5 * math.log2(math.e)
    qt, k, vt, qn2, kmax2 = _qkv_proj(cq, ckv, kr, cst, snt, wqt, wk, wvt, q_scale=q_scale)
    o_bounded, denom = _attention_bounded(qt, qn2, kmax2, k, vt)
    o = lax.cond(jnp.all(denom >= L_MIN), lambda: o_bounded, lambda: _attention(qt, k, vt))

    merged = _merge(hn, hc, o, w_gc, w_gm, bf(conv_w_pw_out), bf(mla_w_o))
    x2 = _out_proj(x1, merged, bf(w_out))
    (y,) = _ffn(x2, row(ffn2_norm_g), bf(ffn2_w_gate), bf(ffn2_w_up), bf(ffn2_w_down),
                row(out_norm_g), final=final)
    return y


def kernel(x, positions, ffn1_norm_g, ffn1_w_gate, ffn1_w_up, ffn1_w_down, mix_norm_g, w_in, conv_w_dw, conv_b_dw, conv_ln_g, conv_ln_b, conv_w_pw_out, mla_q_norm_g, mla_w_uq, mla_kv_norm_g, mla_w_ukv, mla_w_o, w_out, ffn2_norm_g, ffn2_w_gate, ffn2_w_up, ffn2_w_down, final_norm_g):
    B, S, D = x.shape
    depth = ffn1_norm_g.shape[0]
    assert depth == 1, "the fused final norm assumes a single layer"
    inv_freq = ROPE_THETA ** (-jnp.arange(0, QK_ROPE_DIM, 2, dtype=F32) / QK_ROPE_DIM)
    freq = inv_freq.reshape(QK_ROPE_DIM // 2, 1)
    outs = []
    for b in range(B):
        y = _layer(x[b], positions[b].reshape(1, S), freq,
                   ffn1_norm_g[0], ffn1_w_gate[0], ffn1_w_up[0], ffn1_w_down[0], mix_norm_g[0],
                   w_in[0], conv_w_dw[0], conv_b_dw[0], conv_ln_g[0], conv_ln_b[0],
                   conv_w_pw_out[0], mla_q_norm_g[0], mla_w_uq[0], mla_kv_norm_g[0], mla_w_ukv[0],
                   mla_w_o[0], w_out[0], ffn2_norm_g[0], ffn2_w_gate[0], ffn2_w_up[0],
                   ffn2_w_down[0], final_norm_g, final=True)
        outs.append(y)
    return jnp.stack(outs)
```

```python
import functools
import math

import jax
import jax.numpy as jnp
from jax import lax
from jax.experimental import pallas as pl
from jax.experimental.pallas import tpu as pltpu

N_HEADS = 16
QK_NOPE_DIM = 128
QK_ROPE_DIM = 64
V_HEAD_DIM = 128
Q_LORA_RANK = 768
KV_LORA_RANK = 512
CONV_WIDTH = 31
CONV_PAD = CONV_WIDTH // 2
ROPE_THETA = 10000.0
NORM_EPS = 1e-6

LANES = 128
SUBLANES = 8
QK_PAD_DIM = 256
QK_DIM = QK_NOPE_DIM + QK_ROPE_DIM
V_EXT_DIM = V_HEAD_DIM + 16
BOUND_MARGIN = 1.0 + 2.0 ** -6
L_MIN = 2.0 ** -80
HALO_ROWS = 16
VMEM_LIMIT = 56 * 1024 * 1024

F32 = jnp.float32
BF16 = jnp.bfloat16


def _params(*sem):
    return pltpu.CompilerParams(dimension_semantics=sem, vmem_limit_bytes=VMEM_LIMIT)


def _rms(x, g):
    return x * lax.rsqrt(jnp.mean(x * x, axis=-1, keepdims=True) + NORM_EPS) * g


def _ffn_kernel(x_ref, g_ref, wg_ref, wu_ref, wd_ref, g2_ref, *refs, final):
    if final:
        o_ref, hn_scr, acc_scr = refs
    else:
        o_ref, hn_out_ref, hn_scr, acc_scr = refs
    j = pl.program_id(1)

    @pl.when(j == 0)
    def _():
        hn_scr[...] = _rms(x_ref[...], g_ref[...]).astype(BF16)
        acc_scr[...] = jnp.zeros_like(acc_scr)

    h = hn_scr[...]
    a = jnp.dot(h, wg_ref[...], preferred_element_type=F32)
    u = jnp.dot(h, wu_ref[...], preferred_element_type=F32)
    act = (a * jax.nn.sigmoid(a) * u).astype(BF16)
    acc_scr[...] += jnp.dot(act, wd_ref[...], preferred_element_type=F32)

    @pl.when(j == pl.num_programs(1) - 1)
    def _():
        y = x_ref[...] + 0.5 * acc_scr[...]
        if final:
            o_ref[...] = _rms(y, g2_ref[...])
        else:
            o_ref[...] = y
            hn_out_ref[...] = _rms(y, g2_ref[...]).astype(BF16)


def _ffn(x, g, wg, wu, wd, g2, *, final, tm=512, tf=512):
    S, D = x.shape
    F = wg.shape[1]
    tm = min(tm, S)
    row = pl.BlockSpec((tm, D), lambda i, j: (i, 0))
    vec = pl.BlockSpec((1, D), lambda i, j: (0, 0))
    out_shape = [jax.ShapeDtypeStruct((S, D), F32)]
    out_specs = [row]
    if not final:
        out_shape.append(jax.ShapeDtypeStruct((S, D), BF16))
        out_specs.append(row)
    return pl.pallas_call(
        functools.partial(_ffn_kernel, final=final),
        grid=(S // tm, F // tf),
        in_specs=[row, vec,
                  pl.BlockSpec((D, tf), lambda i, j: (0, j)),
                  pl.BlockSpec((D, tf), lambda i, j: (0, j)),
                  pl.BlockSpec((tf, D), lambda i, j: (j, 0)),
                  vec],
        out_specs=out_specs,
        out_shape=out_shape,
        scratch_shapes=[pltpu.VMEM((tm, D), BF16), pltpu.VMEM((tm, D), F32)],
        compiler_params=_params("parallel", "arbitrary"),
        name="ffn_final" if final else "ffn_mix",
    )(x, g, wg, wu, wd, g2)


def _glu_kernel(h_ref, wa_ref, wg_ref, o_ref):
    h = h_ref[...]
    a = jnp.dot(h, wa_ref[...], preferred_element_type=F32)
    g = jnp.dot(h, wg_ref[...], preferred_element_type=F32)
    o_ref[...] = a * jax.nn.sigmoid(g)


def _glu_proj(hn, w_glu, *, tm=1024, tn=1024):
    S, D = hn.shape
    C = w_glu.shape[1] // 2
    tm = min(tm, S)
    nj = C // tn
    return pl.pallas_call(
        _glu_kernel,
        grid=(S // tm, nj),
        in_specs=[pl.BlockSpec((tm, D), lambda i, j: (i, 0)),
                  pl.BlockSpec((D, tn), lambda i, j: (0, j)),
                  pl.BlockSpec((D, tn), lambda i, j: (0, j + nj))],
        out_specs=pl.BlockSpec((tm, tn), lambda i, j: (i, j)),
        out_shape=jax.ShapeDtypeStruct((S, C), F32),
        compiler_params=_params("parallel", "arbitrary"),
        name="glu_proj",
    )(hn, w_glu, w_glu)


def _small_kernel(h_ref, w_ref, gq_ref, gkv_ref, pos_ref, freq_ref,
                  cq_ref, ckv_ref, kr_ref, cst_ref, snt_ref):
    r = jnp.dot(h_ref[...], w_ref[...], preferred_element_type=F32)
    cq_ref[...] = _rms(r[:, :Q_LORA_RANK], gq_ref[...]).astype(BF16)
    c0 = Q_LORA_RANK
    ckv_ref[...] = _rms(r[:, c0:c0 + KV_LORA_RANK], gkv_ref[...]).astype(BF16)
    c1 = c0 + KV_LORA_RANK
    ang = freq_ref[...] * pos_ref[...].astype(F32)
    cos, sin = jnp.cos(ang), jnp.sin(ang)
    cst = jnp.concatenate([cos, cos], axis=0)
    snt = jnp.concatenate([sin, sin], axis=0)
    cst_ref[...] = cst
    snt_ref[...] = snt
    pad = LANES - QK_ROPE_DIM
    cs = jnp.concatenate([cst, jnp.ones((pad, cst.shape[1]), F32)], axis=0).T
    sn = jnp.concatenate([snt, jnp.zeros((pad, snt.shape[1]), F32)], axis=0).T
    kr = r[:, c1:c1 + LANES] * cs + r[:, c1 + LANES:c1 + 2 * LANES] * sn
    lane = lax.broadcasted_iota(jnp.int32, kr.shape, 1)
    kr_ref[...] = jnp.where(lane == QK_ROPE_DIM, 1.0, kr).astype(BF16)


def _small_proj(hn, w_small, gq, gkv, pos, freq, *, tm=512):
    S, D = hn.shape
    tm = min(tm, S)
    row = lambda n: pl.BlockSpec((tm, n), lambda i: (i, 0))
    col = lambda n: pl.BlockSpec((n, tm), lambda i: (0, i))
    full = lambda a: pl.BlockSpec(a.shape, lambda i: (0, 0))
    return pl.pallas_call(
        _small_kernel,
        grid=(S // tm,),
        in_specs=[row(D), full(w_small), full(gq), full(gkv), col(1), full(freq)],
        out_specs=[row(Q_LORA_RANK), row(KV_LORA_RANK), row(LANES), col(QK_ROPE_DIM), col(QK_ROPE_DIM)],
        out_shape=[jax.ShapeDtypeStruct((S, Q_LORA_RANK), BF16),
                   jax.ShapeDtypeStruct((S, KV_LORA_RANK), BF16),
                   jax.ShapeDtypeStruct((S, LANES), BF16),
                   jax.ShapeDtypeStruct((QK_ROPE_DIM, S), F32),
                   jax.ShapeDtypeStruct((QK_ROPE_DIM, S), F32)],
        compiler_params=_params("parallel"),
        name="small_proj",
    )(hn, w_small, gq, gkv, pos, freq)


_NT = (((1,), (1,)), ((), ()))


def _sq(x):
    x = x.astype(F32)
    return x * x


def _qkv_kernel(cq_ref, ckv_ref, kr_ref, cst_ref, snt_ref, wqt_ref, wk_ref, wvt_ref,
                qt_ref, k_ref, vt_ref, qn2_ref, kmax2_ref, *, q_scale):
    @pl.when(pl.program_id(0) == 0)
    def _():
        kmax2_ref[...] = jnp.zeros_like(kmax2_ref)

    cq = cq_ref[...]
    ckv = ckv_ref[...]
    kr = kr_ref[...]
    cst = cst_ref[...]
    snt = snt_ref[...]
    kr_n2 = jnp.sum(_sq(kr), axis=1, keepdims=True) - 1.0
    rt_all = lax.dot_general(wqt_ref[...], cq, _NT, preferred_element_type=F32)
    k_all = jnp.dot(ckv, wk_ref[...], preferred_element_type=F32)
    vt_all = lax.dot_general(wvt_ref[...], ckv, _NT, preferred_element_type=F32)
    for h in range(N_HEADS):
        rt = rt_all[h * QK_PAD_DIM:(h + 1) * QK_PAD_DIM]
        n0, n1, n2 = QK_NOPE_DIM, QK_DIM, QK_DIM + QK_ROPE_DIM
        q_nope = (rt[0:n0] * q_scale).astype(BF16)
        q_rope = ((rt[n0:n1] * cst + rt[n1:n2] * snt) * q_scale).astype(BF16)
        qt_ref[h, 0:n0, :] = q_nope
        qt_ref[h, n0:n1, :] = q_rope
        qt_ref[h, n1:QK_PAD_DIM, :] = jnp.zeros((QK_PAD_DIM - n1, rt.shape[1]), BF16)
        k_nope = k_all[:, h * QK_NOPE_DIM:(h + 1) * QK_NOPE_DIM].astype(BF16)
        k_ref[h, :, 0:LANES] = k_nope
        k_ref[h, :, LANES:2 * LANES] = kr
        vt = vt_all[h * V_HEAD_DIM:(h + 1) * V_HEAD_DIM]
        vt_ref[h, 0:V_HEAD_DIM, :] = vt.astype(BF16)
        vt_ref[h, V_HEAD_DIM:V_EXT_DIM, :] = jnp.ones((V_EXT_DIM - V_HEAD_DIM, vt.shape[1]), BF16)
        qn2_ref[h] = jnp.sum(_sq(q_nope), axis=0, keepdims=True) + jnp.sum(_sq(q_rope), axis=0, keepdims=True)
        k_n2 = jnp.sum(_sq(k_nope), axis=1, keepdims=True) + kr_n2
        kmax2_ref[h] = jnp.maximum(kmax2_ref[h], jnp.max(k_n2, axis=0, keepdims=True))


def _qkv_proj(cq, ckv, kr, cst, snt, wqt, wk, wvt, *, q_scale, tm=256):
    S = cq.shape[0]
    tm = min(tm, S)
    row = lambda n: pl.BlockSpec((tm, n), lambda i: (i, 0))
    col = lambda n: pl.BlockSpec((n, tm), lambda i: (0, i))
    full2 = lambda a: pl.BlockSpec(a.shape, lambda i: (0, 0))
    return pl.pallas_call(
        functools.partial(_qkv_kernel, q_scale=q_scale),
        grid=(S // tm,),
        in_specs=[row(Q_LORA_RANK), row(KV_LORA_RANK), row(LANES), col(QK_ROPE_DIM), col(QK_ROPE_DIM),
                  full2(wqt), full2(wk), full2(wvt)],
        out_specs=[pl.BlockSpec((N_HEADS, QK_PAD_DIM, tm), lambda i: (0, 0, i)),
                   pl.BlockSpec((N_HEADS, tm, QK_PAD_DIM), lambda i: (0, i, 0)),
                   pl.BlockSpec((N_HEADS, V_EXT_DIM, tm), lambda i: (0, 0, i)),
                   pl.BlockSpec((N_HEADS, 1, tm), lambda i: (0, 0, i)),
                   pl.BlockSpec((N_HEADS, SUBLANES, LANES), lambda i: (0, 0, 0))],
        out_shape=[jax.ShapeDtypeStruct((N_HEADS, QK_PAD_DIM, S), BF16),
                   jax.ShapeDtypeStruct((N_HEADS, S, QK_PAD_DIM), BF16),
                   jax.ShapeDtypeStruct((N_HEADS, V_EXT_DIM, S), BF16),
                   jax.ShapeDtypeStruct((N_HEADS, 1, S), F32),
                   jax.ShapeDtypeStruct((N_HEADS, SUBLANES, LANES), F32)],
        compiler_params=_params("arbitrary"),
        name="qkv_proj",
    )(cq, ckv, kr, cst, snt, wqt, wk, wvt)


def _attn_bounded_kernel(qt_ref, qn2_ref, kmax2_ref, k_ref, vt_ref, o_ref, l_ref, *, tk, unroll):
    tq = qt_ref.shape[1]
    n_kv = k_ref.shape[0] // tk
    bound = jnp.sqrt(qn2_ref[...] * kmax2_ref[0:1, 0:1]) * BOUND_MARGIN
    tile = 2 * SUBLANES
    row = lax.broadcasted_iota(jnp.int32, (tile, tq), 0)
    offset_rows = jnp.where(row == 0, -bound, 0.0).astype(BF16)
    qt = jnp.concatenate([qt_ref[0:QK_DIM, :], offset_rows, qt_ref[QK_DIM + tile:, :]], axis=0)

    def body(u, carry):
        acc, l = carry
        for j in range(unroll):
            start = pl.multiple_of((unroll * u + j) * tk, tk)
            s = jnp.dot(k_ref[pl.ds(start, tk), :], qt, preferred_element_type=F32)
            p = jnp.exp2(s)
            l = l + jnp.sum(p, axis=0, keepdims=True)
            acc = acc + jnp.dot(vt_ref[:, pl.ds(start, tk)], p.astype(BF16), preferred_element_type=F32)
        return acc, l

    zeros = (jnp.zeros((V_HEAD_DIM, tq), F32), jnp.zeros((1, tq), F32))
    acc, l = lax.fori_loop(0, n_kv // unroll, body, zeros)
    l_ref[...] = l
    o_ref[...] = (acc / l).T.astype(BF16)


def _attention_bounded(qt, qn2, kmax2, k, vt, *, tq=1024, tk=2048, unroll=8):
    H, S, _ = k.shape
    tq = min(tq, S)
    tk = min(tk, S // unroll)
    assert S % (unroll * tk) == 0 and S % tq == 0
    return pl.pallas_call(
        functools.partial(_attn_bounded_kernel, tk=tk, unroll=unroll),
        grid=(H, S // tq),
        in_specs=[pl.BlockSpec((None, QK_PAD_DIM, tq), lambda h, i: (h, 0, i)),
                  pl.BlockSpec((None, 1, tq), lambda h, i: (h, 0, i)),
                  pl.BlockSpec((None, SUBLANES, LANES), lambda h, i: (h, 0, 0)),
                  pl.BlockSpec((None, S, QK_PAD_DIM), lambda h, i: (h, 0, 0)),
                  pl.BlockSpec((None, V_HEAD_DIM, S), lambda h, i: (h, 0, 0))],
        out_specs=[pl.BlockSpec((tq, V_HEAD_DIM), lambda h, i: (i, h)),
                   pl.BlockSpec((None, 1, tq), lambda h, i: (h, 0, i))],
        out_shape=[jax.ShapeDtypeStruct((S, H * V_HEAD_DIM), BF16),
                   jax.ShapeDtypeStruct((H, 1, S), F32)],
        compiler_params=_params("parallel", "arbitrary"),
        name="attention_bounded",
    )(qt, qn2, kmax2, k, vt)


def _attn_kernel(qt_ref, k_ref, vt_ref, o_ref, s_scr, *, tk, unroll):
    qt = qt_ref[...]
    tq = qt.shape[1]
    n_kv = k_ref.shape[0] // tk

    def scores(t, slot):
        start = pl.multiple_of(t * tk, tk)
        s = jnp.dot(k_ref[pl.ds(start, tk), :], qt, preferred_element_type=F32)
        s_scr[slot] = s
        return jnp.max(s, axis=0, keepdims=True)

    def softmax_pv(slot, s_max, t, m, acc):
        m_new = jnp.maximum(m, s_max)
        alpha = jnp.exp2(m - m_new)
        p = jnp.exp2(s_scr[slot] - m_new)
        start = pl.multiple_of(t * tk, tk)
        pv = jnp.dot(vt_ref[:, pl.ds(start, tk)], p.astype(BF16), preferred_element_type=F32)
        return m_new, alpha * acc + pv

    def body(u, carry):
        m, acc, s_max = carry
        t = unroll * u
        for j in range(unroll):
            nxt = t + j + 1
            if j == unroll - 1:
                nxt = jnp.minimum(nxt, n_kv - 1)
            next_max = scores(nxt, (j + 1) % 2)
            m, acc = softmax_pv(j % 2, s_max, t + j, m, acc)
            s_max = next_max
        return m, acc, s_max

    m0 = jnp.full((1, tq), -jnp.inf, F32)
    acc0 = jnp.zeros((V_EXT_DIM, tq), F32)
    _, acc, _ = lax.fori_loop(0, n_kv // unroll, body, (m0, acc0, scores(0, 0)))
    o_ref[...] = (acc[:V_HEAD_DIM] / acc[V_HEAD_DIM:V_HEAD_DIM + 1]).T.astype(BF16)


def _attention(qt, k, vt, *, tq=512, tk=512, unroll=16):
    H, S, _ = k.shape
    tq = min(tq, S)
    tk = min(tk, S // unroll)
    assert unroll % 2 == 0 and S % (unroll * tk) == 0 and S % tq == 0
    return pl.pallas_call(
        functools.partial(_attn_kernel, tk=tk, unroll=unroll),
        grid=(H, S // tq),
        in_specs=[pl.BlockSpec((None, QK_PAD_DIM, tq), lambda h, i: (h, 0, i)),
                  pl.BlockSpec((None, S, QK_PAD_DIM), lambda h, i: (h, 0, 0)),
                  pl.BlockSpec((None, V_EXT_DIM, S), lambda h, i: (h, 0, 0))],
        out_specs=pl.BlockSpec((tq, V_HEAD_DIM), lambda h, i: (i, h)),
        out_shape=jax.ShapeDtypeStruct((S, H * V_HEAD_DIM), BF16),
        scratch_shapes=[pltpu.VMEM((2, tk, tq), F32)],
        compiler_params=_params("parallel", "arbitrary"),
        name="attention",
    )(qt, k, vt)


def _conv_kernel(prev_ref, cur_ref, next_ref, w_ref, b_ref, g_ref, beta_ref, o_ref,
                 buf, shifted, conv_scr, *, rb, lc):
    i = pl.program_id(0)
    ts, C = cur_ref.shape
    first = i == 0
    last = i == pl.num_programs(0) - 1
    buf[0:HALO_ROWS, :] = jnp.where(first, 0.0, prev_ref[...])
    buf[HALO_ROWS:HALO_ROWS + ts, :] = cur_ref[...]
    buf[HALO_ROWS + ts:, :] = jnp.where(last, 0.0, next_ref[...])
    off = HALO_ROWS - CONV_PAD
    n_sh = ts + SUBLANES * ((CONV_WIDTH + off - 1) // SUBLANES)

    for c0 in range(0, C, lc):
        for b in range(1, SUBLANES):
            shifted[b - 1, 0:n_sh, :] = buf[b:b + n_sh, c0:c0 + lc]

        def rows(rblk, _, c0=c0):
            r0 = pl.multiple_of(rblk * rb, rb)
            acc = jnp.broadcast_to(b_ref[:, c0:c0 + lc], (rb, lc))
            for k in range(CONV_WIDTH):
                a, b = divmod(k + off, SUBLANES)
                start = pl.multiple_of(r0 + a * SUBLANES, SUBLANES)
                if b == 0:
                    tap = buf[pl.ds(start, rb), c0:c0 + lc]
                else:
                    tap = shifted[b - 1, pl.ds(start, rb), :]
                acc = acc + w_ref[k:k + 1, c0:c0 + lc] * tap
            conv_scr[pl.ds(r0, rb), c0:c0 + lc] = acc
            return 0
        lax.fori_loop(0, ts // rb, rows, 0)

    y = conv_scr[...]
    mu = jnp.mean(y, axis=-1, keepdims=True)
    yc = y - mu
    var = jnp.mean(yc * yc, axis=-1, keepdims=True)
    z = yc * lax.rsqrt(var + NORM_EPS) * g_ref[...] + beta_ref[...]
    o_ref[...] = (z * jax.nn.sigmoid(z)).astype(BF16)


def _conv(hglu, w, b, g, beta, *, ts=256, rb=128, lc=128):
    S, C = hglu.shape
    ts = min(ts, S)
    n = S // ts
    hb = ts // HALO_ROWS
    last_halo = S // HALO_ROWS - 1
    vec = pl.BlockSpec((1, C), lambda i: (0, 0))
    return pl.pallas_call(
        functools.partial(_conv_kernel, rb=rb, lc=lc),
        grid=(n,),
        in_specs=[pl.BlockSpec((HALO_ROWS, C), lambda i: (jnp.maximum(i * hb - 1, 0), 0)),
                  pl.BlockSpec((ts, C), lambda i: (i, 0)),
                  pl.BlockSpec((HALO_ROWS, C), lambda i: (jnp.minimum((i + 1) * hb, last_halo), 0)),
                  pl.BlockSpec((CONV_WIDTH, C), lambda i: (0, 0)),
                  vec, vec, vec],
        out_specs=pl.BlockSpec((ts, C), lambda i: (i, 0)),
        out_shape=jax.ShapeDtypeStruct((S, C), BF16),
        scratch_shapes=[pltpu.VMEM((ts + 2 * HALO_ROWS, C), F32),
                        pltpu.VMEM((SUBLANES - 1, ts + 2 * HALO_ROWS, lc), F32),
                        pltpu.VMEM((ts, C), F32)],
        compiler_params=_params("parallel"),
        name="conv",
    )(hglu, hglu, hglu, w, b, g, beta)


def _merge_kernel(hn_ref, hc_ref, o_ref, wgc_ref, wgm_ref, wpw_ref, wo_ref, out_ref):
    hn = hn_ref[...]
    gc = jax.nn.sigmoid(jnp.dot(hn, wgc_ref[...], preferred_element_type=F32))
    gm = jax.nn.sigmoid(jnp.dot(hn, wgm_ref[...], preferred_element_type=F32))
    yc = jnp.dot(hc_ref[...], wpw_ref[...], preferred_element_type=F32)
    ym = jnp.dot(o_ref[...], wo_ref[...], preferred_element_type=F32)
    out_ref[...] = (gc * yc + gm * ym).astype(BF16)


def _merge(hn, hc, o, wgc, wgm, wpw, wo, *, tm=1024, tn=512):
    S, D = hn.shape
    tm = min(tm, S)
    row = pl.BlockSpec((tm, D), lambda i, j: (i, 0))
    col = pl.BlockSpec((D, tn), lambda i, j: (0, j))
    return pl.pallas_call(
        _merge_kernel,
        grid=(S // tm, D // tn),
        in_specs=[row, row, row, col, col, col, col],
        out_specs=pl.BlockSpec((tm, tn), lambda i, j: (i, j)),
        out_shape=jax.ShapeDtypeStruct((S, D), BF16),
        compiler_params=_params("parallel", "arbitrary"),
        name="merge",
    )(hn, hc, o, wgc, wgm, wpw, wo)


def _outproj_kernel(x_ref, m_ref, w_ref, o_ref):
    o_ref[...] = x_ref[...] + jnp.dot(m_ref[...], w_ref[...], preferred_element_type=F32)


def _out_proj(x, merged, w, *, tm=512):
    S, D = x.shape
    tm = min(tm, S)
    row = pl.BlockSpec((tm, D), lambda i: (i, 0))
    return pl.pallas_call(
        _outproj_kernel,
        grid=(S // tm,),
        in_specs=[row, row, pl.BlockSpec((D, D), lambda i: (0, 0))],
        out_specs=row,
        out_shape=jax.ShapeDtypeStruct((S, D), F32),
        compiler_params=_params("parallel"),
        name="out_proj",
    )(x, merged, w)


def _rot_half_cols(w):
    half = QK_ROPE_DIM // 2
    return jnp.concatenate([-w[..., half:], w[..., :half]], axis=-1)


def _pad_cols(w, n):
    return jnp.pad(w, [(0, 0)] * (w.ndim - 1) + [(0, n - w.shape[-1])])


def _layer(x, pos, freq, ffn1_norm_g, ffn1_w_gate, ffn1_w_up, ffn1_w_down, mix_norm_g, w_in,
           conv_w_dw, conv_b_dw, conv_ln_g, conv_ln_b, conv_w_pw_out, mla_q_norm_g, mla_w_uq,
           mla_kv_norm_g, mla_w_ukv, mla_w_o, w_out, ffn2_norm_g, ffn2_w_gate, ffn2_w_up,
           ffn2_w_down, out_norm_g, *, final):
    D = x.shape[1]
    C = conv_w_dw.shape[1]
    row = lambda v: v.reshape(1, -1)
    bf = lambda w: w.astype(BF16)

    o_q = 2 * C
    o_kv = o_q + Q_LORA_RANK
    o_kr = o_kv + KV_LORA_RANK
    o_gc = o_kr + QK_ROPE_DIM
    o_gm = o_gc + D
    w_in = lax.optimization_barrier(bf(w_in))
    w_glu = w_in[:, :o_q]
    w_kr = w_in[:, o_kr:o_gc]
    w_small = jnp.concatenate(
        [w_in[:, o_q:o_kr], _pad_cols(w_kr, LANES), _pad_cols(_rot_half_cols(w_kr), LANES)], axis=1)
    w_gc = w_in[:, o_gc:o_gm]
    w_gm = w_in[:, o_gm:]

    wq = lax.optimization_barrier(bf(mla_w_uq)).reshape(Q_LORA_RANK, N_HEADS, QK_NOPE_DIM + QK_ROPE_DIM)
    wq_rope = wq[..., QK_NOPE_DIM:]
    wq = jnp.concatenate([wq[..., :QK_NOPE_DIM], wq_rope, _rot_half_cols(wq_rope)], axis=-1)
    wqt = wq.transpose(1, 2, 0).reshape(N_HEADS * QK_PAD_DIM, Q_LORA_RANK)
    wkv = lax.optimization_barrier(bf(mla_w_ukv)).reshape(KV_LORA_RANK, N_HEADS, QK_NOPE_DIM + V_HEAD_DIM)
    wk = wkv[..., :QK_NOPE_DIM].reshape(KV_LORA_RANK, N_HEADS * QK_NOPE_DIM)
    wvt = wkv[..., QK_NOPE_DIM:].transpose(1, 2, 0).reshape(N_HEADS * V_HEAD_DIM, KV_LORA_RANK)

    x1, hn = _ffn(x, row(ffn1_norm_g), bf(ffn1_w_gate), bf(ffn1_w_up), bf(ffn1_w_down),
                  row(mix_norm_g), final=False)

    hglu = _glu_proj(hn, w_glu)
    hc = _conv(hglu, conv_w_dw, row(conv_b_dw), row(conv_ln_g), row(conv_ln_b))

    cq, ckv, kr, cst, snt = _small_proj(hn, w_small, row(mla_q_norm_g), row(mla_kv_norm_g), pos, freq)
    q_scale = (QK_NOPE_DIM + QK_ROPE_DIM) ** -0.5 * math.log2(math.e)
    qt, k, vt, qn2, kmax2 = _qkv_proj(cq, ckv, kr, cst, snt, wqt, wk, wvt, q_scale=q_scale)
    o_bounded, denom = _attention_bounded(qt, qn2, kmax2, k, vt)
    o = lax.cond(jnp.all(denom >= L_MIN), lambda: o_bounded, lambda: _attention(qt, k, vt))

    merged = _merge(hn, hc, o, w_gc, w_gm, bf(conv_w_pw_out), bf(mla_w_o))
    x2 = _out_proj(x1, merged, bf(w_out))
    (y,) = _ffn(x2, row(ffn2_norm_g), bf(ffn2_w_gate), bf(ffn2_w_up), bf(ffn2_w_down),
                row(out_norm_g), final=final)
    return y


def kernel(x, positions, ffn1_norm_g, ffn1_w_gate, ffn1_w_up, ffn1_w_down, mix_norm_g, w_in, conv_w_dw, conv_b_dw, conv_ln_g, conv_ln_b, conv_w_pw_out, mla_q_norm_g, mla_w_uq, mla_kv_norm_g, mla_w_ukv, mla_w_o, w_out, ffn2_norm_g, ffn2_w_gate, ffn2_w_up, ffn2_w_down, final_norm_g):
    B, S, D = x.shape
    depth = ffn1_norm_g.shape[0]
    assert depth == 1, "the fused final norm assumes a single layer"
    inv_freq = ROPE_THETA ** (-jnp.arange(0, QK_ROPE_DIM, 2, dtype=F32) / QK_ROPE_DIM)
    freq = inv_freq.reshape(QK_ROPE_DIM // 2, 1)
    outs = []
    for b in range(B):
        y = _layer(x[b], positions[b].reshape(1, S), freq,
                   ffn1_norm_g[0], ffn1_w_gate[0], ffn1_w_up[0], ffn1_w_down[0], mix_norm_g[0],
                   w_in[0], conv_w_dw[0], conv_b_dw[0], conv_ln_g[0], conv_ln_b[0],
                   conv_w_pw_out[0], mla_q_norm_g[0], mla_w_uq[0], mla_kv_norm_g[0], mla_w_ukv[0],
                   mla_w_o[0], w_out[0], ffn2_norm_g[0], ffn2_w_gate[0], ffn2_w_up[0],
                   ffn2_w_down[0], final_norm_g, final=True)
        outs.append(y)
    return jnp.stack(outs)
```

```python
import functools
import math

import jax
import jax.numpy as jnp
from jax import lax
from jax.experimental import pallas as pl
from jax.experimental.pallas import tpu as pltpu

N_HEADS = 16
QK_NOPE_DIM = 128
QK_ROPE_DIM = 64
V_HEAD_DIM = 128
Q_LORA_RANK = 768
KV_LORA_RANK = 512
CONV_WIDTH = 31
CONV_PAD = CONV_WIDTH // 2
ROPE_THETA = 10000.0
NORM_EPS = 1e-6

LANES = 128
SUBLANES = 8
QK_PAD_DIM = 256
QK_DIM = QK_NOPE_DIM + QK_ROPE_DIM
V_EXT_DIM = V_HEAD_DIM + 16
BOUND_MARGIN = 1.0 + 2.0 ** -6
L_MIN = 2.0 ** -80
HALO_ROWS = 16
VMEM_LIMIT = 56 * 1024 * 1024

F32 = jnp.float32
BF16 = jnp.bfloat16


def _params(*sem):
    return pltpu.CompilerParams(dimension_semantics=sem, vmem_limit_bytes=VMEM_LIMIT)


def _rms(x, g):
    return x * lax.rsqrt(jnp.mean(x * x, axis=-1, keepdims=True) + NORM_EPS) * g


def _ffn_kernel(x_ref, g_ref, wg_ref, wu_ref, wd_ref, g2_ref, *refs, final):
    if final:
        o_ref, hn_scr, acc_scr = refs
    else:
        o_ref, hn_out_ref, hn_scr, acc_scr = refs
    j = pl.program_id(1)

    @pl.when(j == 0)
    def _():
        hn_scr[...] = _rms(x_ref[...], g_ref[...]).astype(BF16)
        acc_scr[...] = jnp.zeros_like(acc_scr)

    h = hn_scr[...]
    half = wg_ref.shape[1] // 2
    acts = []
    for c in (0, half):
        a = jnp.dot(h, wg_ref[:, c:c + half], preferred_element_type=F32)
        u = jnp.dot(h, wu_ref[:, c:c + half], preferred_element_type=F32)
        acts.append((a * jax.nn.sigmoid(a) * u).astype(BF16))
    act = jnp.concatenate(acts, axis=1)
    acc_scr[...] += jnp.dot(act, wd_ref[...], preferred_element_type=F32)

    @pl.when(j == pl.num_programs(1) - 1)
    def _():
        y = x_ref[...] + 0.5 * acc_scr[...]
        if final:
            o_ref[...] = _rms(y, g2_ref[...])
        else:
            o_ref[...] = y
            hn_out_ref[...] = _rms(y, g2_ref[...]).astype(BF16)


def _ffn(x, g, wg, wu, wd, g2, *, final, tm=512, tf=512):
    S, D = x.shape
    F = wg.shape[1]
    tm = min(tm, S)
    row = pl.BlockSpec((tm, D), lambda i, j: (i, 0))
    vec = pl.BlockSpec((1, D), lambda i, j: (0, 0))
    out_shape = [jax.ShapeDtypeStruct((S, D), F32)]
    out_specs = [row]
    if not final:
        out_shape.append(jax.ShapeDtypeStruct((S, D), BF16))
        out_specs.append(row)
    return pl.pallas_call(
        functools.partial(_ffn_kernel, final=final),
        grid=(S // tm, F // tf),
        in_specs=[row, vec,
                  pl.BlockSpec((D, tf), lambda i, j: (0, j)),
                  pl.BlockSpec((D, tf), lambda i, j: (0, j)),
                  pl.BlockSpec((tf, D), lambda i, j: (j, 0)),
                  vec],
        out_specs=out_specs,
        out_shape=out_shape,
        scratch_shapes=[pltpu.VMEM((tm, D), BF16), pltpu.VMEM((tm, D), F32)],
        compiler_params=_params("parallel", "arbitrary"),
        name="ffn_final" if final else "ffn_mix",
    )(x, g, wg, wu, wd, g2)


def _glu_kernel(h_ref, wa_ref, wg_ref, o_ref):
    h = h_ref[...]
    a = jnp.dot(h, wa_ref[...], preferred_element_type=F32)
    g = jnp.dot(h, wg_ref[...], preferred_element_type=F32)
    o_ref[...] = a * jax.nn.sigmoid(g)


def _glu_proj(hn, w_glu, *, tm=1024, tn=1024):
    S, D = hn.shape
    C = w_glu.shape[1] // 2
    tm = min(tm, S)
    nj = C // tn
    return pl.pallas_call(
        _glu_kernel,
        grid=(S // tm, nj),
        in_specs=[pl.BlockSpec((tm, D), lambda i, j: (i, 0)),
                  pl.BlockSpec((D, tn), lambda i, j: (0, j)),
                  pl.BlockSpec((D, tn), lambda i, j: (0, j + nj))],
        out_specs=pl.BlockSpec((tm, tn), lambda i, j: (i, j)),
        out_shape=jax.ShapeDtypeStruct((S, C), F32),
        compiler_params=_params("parallel", "arbitrary"),
        name="glu_proj",
    )(hn, w_glu, w_glu)


def _small_kernel(h_ref, w_ref, gq_ref, gkv_ref, pos_ref, freq_ref,
                  cq_ref, ckv_ref, kr_ref, cst_ref, snt_ref):
    r = jnp.dot(h_ref[...], w_ref[...], preferred_element_type=F32)
    cq_ref[...] = _rms(r[:, :Q_LORA_RANK], gq_ref[...]).astype(BF16)
    c0 = Q_LORA_RANK
    ckv_ref[...] = _rms(r[:, c0:c0 + KV_LORA_RANK], gkv_ref[...]).astype(BF16)
    c1 = c0 + KV_LORA_RANK
    ang = freq_ref[...] * pos_ref[...].astype(F32)
    cos, sin = jnp.cos(ang), jnp.sin(ang)
    cst = jnp.concatenate([cos, cos], axis=0)
    snt = jnp.concatenate([sin, sin], axis=0)
    cst_ref[...] = cst
    snt_ref[...] = snt
    pad = LANES - QK_ROPE_DIM
    cs = jnp.concatenate([cst, jnp.ones((pad, cst.shape[1]), F32)], axis=0).T
    sn = jnp.concatenate([snt, jnp.zeros((pad, snt.shape[1]), F32)], axis=0).T
    kr = r[:, c1:c1 + LANES] * cs + r[:, c1 + LANES:c1 + 2 * LANES] * sn
    lane = lax.broadcasted_iota(jnp.int32, kr.shape, 1)
    kr_ref[...] = jnp.where(lane == QK_ROPE_DIM, 1.0, kr).astype(BF16)


def _small_proj(hn, w_small, gq, gkv, pos, freq, *, tm=512):
    S, D = hn.shape
    tm = min(tm, S)
    row = lambda n: pl.BlockSpec((tm, n), lambda i: (i, 0))
    col = lambda n: pl.BlockSpec((n, tm), lambda i: (0, i))
    full = lambda a: pl.BlockSpec(a.shape, lambda i: (0, 0))
    return pl.pallas_call(
        _small_kernel,
        grid=(S // tm,),
        in_specs=[row(D), full(w_small), full(gq), full(gkv), col(1), full(freq)],
        out_specs=[row(Q_LORA_RANK), row(KV_LORA_RANK), row(LANES), col(QK_ROPE_DIM), col(QK_ROPE_DIM)],
        out_shape=[jax.ShapeDtypeStruct((S, Q_LORA_RANK), BF16),
                   jax.ShapeDtypeStruct((S, KV_LORA_RANK), BF16),
                   jax.ShapeDtypeStruct((S, LANES), BF16),
                   jax.ShapeDtypeStruct((QK_ROPE_DIM, S), F32),
                   jax.ShapeDtypeStruct((QK_ROPE_DIM, S), F32)],
        compiler_params=_params("parallel"),
        name="small_proj",
    )(hn, w_small, gq, gkv, pos, freq)


_NT = (((1,), (1,)), ((), ()))


def _sq(x):
    x = x.astype(F32)
    return x * x


def _qkv_kernel(cq_ref, ckv_ref, kr_ref, cst_ref, snt_ref, wqt_ref, wk_ref, wvt_ref,
                qt_ref, k_ref, vt_ref, qn2_ref, kmax2_ref, *, q_scale):
    @pl.when(pl.program_id(0) == 0)
    def _():
        kmax2_ref[...] = jnp.zeros_like(kmax2_ref)

    cq = cq_ref[...]
    ckv = ckv_ref[...]
    kr = kr_ref[...]
    cst = cst_ref[...]
    snt = snt_ref[...]
    kr_n2 = jnp.sum(_sq(kr), axis=1, keepdims=True) - 1.0
    rt_all = lax.dot_general(wqt_ref[...], cq, _NT, preferred_element_type=F32)
    k_all = jnp.dot(ckv, wk_ref[...], preferred_element_type=F32)
    vt_all = lax.dot_general(wvt_ref[...], ckv, _NT, preferred_element_type=F32)
    for h in range(N_HEADS):
        rt = rt_all[h * QK_PAD_DIM:(h + 1) * QK_PAD_DIM]
        n0, n1, n2 = QK_NOPE_DIM, QK_DIM, QK_DIM + QK_ROPE_DIM
        q_nope = (rt[0:n0] * q_scale).astype(BF16)
        q_rope = ((rt[n0:n1] * cst + rt[n1:n2] * snt) * q_scale).astype(BF16)
        qt_ref[h, 0:n0, :] = q_nope
        qt_ref[h, n0:n1, :] = q_rope
        qt_ref[h, n1:QK_PAD_DIM, :] = jnp.zeros((QK_PAD_DIM - n1, rt.shape[1]), BF16)
        k_nope = k_all[:, h * QK_NOPE_DIM:(h + 1) * QK_NOPE_DIM].astype(BF16)
        k_ref[h, :, 0:LANES] = k_nope
        k_ref[h, :, LANES:2 * LANES] = kr
        vt = vt_all[h * V_HEAD_DIM:(h + 1) * V_HEAD_DIM]
        vt_ref[h, 0:V_HEAD_DIM, :] = vt.astype(BF16)
        vt_ref[h, V_HEAD_DIM:V_EXT_DIM, :] = jnp.ones((V_EXT_DIM - V_HEAD_DIM, vt.shape[1]), BF16)
        qn2_ref[h] = jnp.sum(_sq(q_nope), axis=0, keepdims=True) + jnp.sum(_sq(q_rope), axis=0, keepdims=True)
        k_n2 = jnp.sum(_sq(k_nope), axis=1, keepdims=True) + kr_n2
        kmax2_ref[h] = jnp.maximum(kmax2_ref[h], jnp.max(k_n2, axis=0, keepdims=True))


def _qkv_proj(cq, ckv, kr, cst, snt, wqt, wk, wvt, *, q_scale, tm=256):
    S = cq.shape[0]
    tm = min(tm, S)
    row = lambda n: pl.BlockSpec((tm, n), lambda i: (i, 0))
    col = lambda n: pl.BlockSpec((n, tm), lambda i: (0, i))
    full2 = lambda a: pl.BlockSpec(a.shape, lambda i: (0, 0))
    return pl.pallas_call(
        functools.partial(_qkv_kernel, q_scale=q_scale),
        grid=(S // tm,),
        in_specs=[row(Q_LORA_RANK), row(KV_LORA_RANK), row(LANES), col(QK_ROPE_DIM), col(QK_ROPE_DIM),
                  full2(wqt), full2(wk), full2(wvt)],
        out_specs=[pl.BlockSpec((N_HEADS, QK_PAD_DIM, tm), lambda i: (0, 0, i)),
                   pl.BlockSpec((N_HEADS, tm, QK_PAD_DIM), lambda i: (0, i, 0)),
                   pl.BlockSpec((N_HEADS, V_EXT_DIM, tm), lambda i: (0, 0, i)),
                   pl.BlockSpec((N_HEADS, 1, tm), lambda i: (0, 0, i)),
                   pl.BlockSpec((N_HEADS, SUBLANES, LANES), lambda i: (0, 0, 0))],
        out_shape=[jax.ShapeDtypeStruct((N_HEADS, QK_PAD_DIM, S), BF16),
                   jax.ShapeDtypeStruct((N_HEADS, S, QK_PAD_DIM), BF16),
                   jax.ShapeDtypeStruct((N_HEADS, V_EXT_DIM, S), BF16),
                   jax.ShapeDtypeStruct((N_HEADS, 1, S), F32),
                   jax.ShapeDtypeStruct((N_HEADS, SUBLANES, LANES), F32)],
        compiler_params=_params("arbitrary"),
        name="qkv_proj",
    )(cq, ckv, kr, cst, snt, wqt, wk, wvt)


def _attn_bounded_kernel(qt_ref, qn2_ref, kmax2_ref, k_ref, vt_ref, o_ref, l_ref, *, tk, unroll):
    tq = qt_ref.shape[1]
    n_kv = k_ref.shape[0] // tk
    bound = jnp.sqrt(qn2_ref[...] * kmax2_ref[0:1, 0:1]) * BOUND_MARGIN
    tile = 2 * SUBLANES
    row = lax.broadcasted_iota(jnp.int32, (tile, tq), 0)
    offset_rows = jnp.where(row == 0, -bound, 0.0).astype(BF16)
    qt = jnp.concatenate([qt_ref[0:QK_DIM, :], offset_rows, qt_ref[QK_DIM + tile:, :]], axis=0)

    def body(u, carry):
        acc, l = carry
        for j in range(unroll):
            start = pl.multiple_of((unroll * u + j) * tk, tk)
            s = jnp.dot(k_ref[pl.ds(start, tk), :], qt, preferred_element_type=F32)
            p = jnp.exp2(s)
            l = l + jnp.sum(p, axis=0, keepdims=True)
            acc = acc + jnp.dot(vt_ref[:, pl.ds(start, tk)], p.astype(BF16), preferred_element_type=F32)
        return acc, l

    zeros = (jnp.zeros((V_HEAD_DIM, tq), F32), jnp.zeros((1, tq), F32))
    acc, l = lax.fori_loop(0, n_kv // unroll, body, zeros)
    l_ref[...] = l
    o_ref[...] = (acc / l).T.astype(BF16)


def _attention_bounded(qt, qn2, kmax2, k, vt, *, tq=1024, tk=2048, unroll=8):
    H, S, _ = k.shape
    tq = min(tq, S)
    tk = min(tk, S // unroll)
    assert S % (unroll * tk) == 0 and S % tq == 0
    return pl.pallas_call(
        functools.partial(_attn_bounded_kernel, tk=tk, unroll=unroll),
        grid=(H, S // tq),
        in_specs=[pl.BlockSpec((None, QK_PAD_DIM, tq), lambda h, i: (h, 0, i)),
                  pl.BlockSpec((None, 1, tq), lambda h, i: (h, 0, i)),
                  pl.BlockSpec((None, SUBLANES, LANES), lambda h, i: (h, 0, 0)),
                  pl.BlockSpec((None, S, QK_PAD_DIM), lambda h, i: (h, 0, 0)),
                  pl.BlockSpec((None, V_HEAD_DIM, S), lambda h, i: (h, 0, 0))],
        out_specs=[pl.BlockSpec((tq, V_HEAD_DIM), lambda h, i: (i, h)),
                   pl.BlockSpec((None, 1, tq), lambda h, i: (h, 0, i))],
        out_shape=[jax.ShapeDtypeStruct((S, H * V_HEAD_DIM), BF16),
                   jax.ShapeDtypeStruct((H, 1, S), F32)],
        compiler_params=_params("parallel", "arbitrary"),
        name="attention_bounded",
    )(qt, qn2, kmax2, k, vt)


def _attn_kernel(qt_ref, k_ref, vt_ref, o_ref, s_scr, *, tk, unroll):
    qt = qt_ref[...]
    tq = qt.shape[1]
    n_kv = k_ref.shape[0] // tk

    def scores(t, slot):
        start = pl.multiple_of(t * tk, tk)
        s = jnp.dot(k_ref[pl.ds(start, tk), :], qt, preferred_element_type=F32)
        s_scr[slot] = s
        return jnp.max(s, axis=0, keepdims=True)

    def softmax_pv(slot, s_max, t, m, acc):
        m_new = jnp.maximum(m, s_max)
        alpha = jnp.exp2(m - m_new)
        p = jnp.exp2(s_scr[slot] - m_new)
        start = pl.multiple_of(t * tk, tk)
        pv = jnp.dot(vt_ref[:, pl.ds(start, tk)], p.astype(BF16), preferred_element_type=F32)
        return m_new, alpha * acc + pv

    def body(u, carry):
        m, acc, s_max = carry
        t = unroll * u
        for j in range(unroll):
            nxt = t + j + 1
            if j == unroll - 1:
                nxt = jnp.minimum(nxt, n_kv - 1)
            next_max = scores(nxt, (j + 1) % 2)
            m, acc = softmax_pv(j % 2, s_max, t + j, m, acc)
            s_max = next_max
        return m, acc, s_max

    m0 = jnp.full((1, tq), -jnp.inf, F32)
    acc0 = jnp.zeros((V_EXT_DIM, tq), F32)
    _, acc, _ = lax.fori_loop(0, n_kv // unroll, body, (m0, acc0, scores(0, 0)))
    o_ref[...] = (acc[:V_HEAD_DIM] / acc[V_HEAD_DIM:V_HEAD_DIM + 1]).T.astype(BF16)


def _attention(qt, k, vt, *, tq=512, tk=512, unroll=16):
    H, S, _ = k.shape
    tq = min(tq, S)
    tk = min(tk, S // unroll)
    assert unroll % 2 == 0 and S % (unroll * tk) == 0 and S % tq == 0
    return pl.pallas_call(
        functools.partial(_attn_kernel, tk=tk, unroll=unroll),
        grid=(H, S // tq),
        in_specs=[pl.BlockSpec((None, QK_PAD_DIM, tq), lambda h, i: (h, 0, i)),
                  pl.BlockSpec((None, S, QK_PAD_DIM), lambda h, i: (h, 0, 0)),
                  pl.BlockSpec((None, V_EXT_DIM, S), lambda h, i: (h, 0, 0))],
        out_specs=pl.BlockSpec((tq, V_HEAD_DIM), lambda h, i: (i, h)),
        out_shape=jax.ShapeDtypeStruct((S, H * V_HEAD_DIM), BF16),
        scratch_shapes=[pltpu.VMEM((2, tk, tq), F32)],
        compiler_params=_params("parallel", "arbitrary"),
        name="attention",
    )(qt, k, vt)


def _conv_kernel(prev_ref, cur_ref, next_ref, w_ref, b_ref, g_ref, beta_ref, o_ref,
                 buf, shifted, conv_scr, *, rb, lc):
    i = pl.program_id(0)
    ts, C = cur_ref.shape
    first = i == 0
    last = i == pl.num_programs(0) - 1
    buf[0:HALO_ROWS, :] = jnp.where(first, 0.0, prev_ref[...])
    buf[HALO_ROWS:HALO_ROWS + ts, :] = cur_ref[...]
    buf[HALO_ROWS + ts:, :] = jnp.where(last, 0.0, next_ref[...])
    off = HALO_ROWS - CONV_PAD
    n_sh = ts + SUBLANES * ((CONV_WIDTH + off - 1) // SUBLANES)

    for c0 in range(0, C, lc):
        for b in range(1, SUBLANES):
            shifted[b - 1, 0:n_sh, :] = buf[b:b + n_sh, c0:c0 + lc]

        def rows(rblk, _, c0=c0):
            r0 = pl.multiple_of(rblk * rb, rb)
            acc = jnp.broadcast_to(b_ref[:, c0:c0 + lc], (rb, lc))
            for k in range(CONV_WIDTH):
                a, b = divmod(k + off, SUBLANES)
                start = pl.multiple_of(r0 + a * SUBLANES, SUBLANES)
                if b == 0:
                    tap = buf[pl.ds(start, rb), c0:c0 + lc]
                else:
                    tap = shifted[b - 1, pl.ds(start, rb), :]
                acc = acc + w_ref[k:k + 1, c0:c0 + lc] * tap
            conv_scr[pl.ds(r0, rb), c0:c0 + lc] = acc
            return 0
        lax.fori_loop(0, ts // rb, rows, 0)

    y = conv_scr[...]
    mu = jnp.mean(y, axis=-1, keepdims=True)
    yc = y - mu
    var = jnp.mean(yc * yc, axis=-1, keepdims=True)
    z = yc * lax.rsqrt(var + NORM_EPS) * g_ref[...] + beta_ref[...]
    o_ref[...] = (z * jax.nn.sigmoid(z)).astype(BF16)


def _conv(hglu, w, b, g, beta, *, ts=256, rb=128, lc=128):
    S, C = hglu.shape
    ts = min(ts, S)
    n = S // ts
    hb = ts // HALO_ROWS
    last_halo = S // HALO_ROWS - 1
    vec = pl.BlockSpec((1, C), lambda i: (0, 0))
    return pl.pallas_call(
        functools.partial(_conv_kernel, rb=rb, lc=lc),
        grid=(n,),
        in_specs=[pl.BlockSpec((HALO_ROWS, C), lambda i: (jnp.maximum(i * hb - 1, 0), 0)),
                  pl.BlockSpec((ts, C), lambda i: (i, 0)),
                  pl.BlockSpec((HALO_ROWS, C), lambda i: (jnp.minimum((i + 1) * hb, last_halo), 0)),
                  pl.BlockSpec((CONV_WIDTH, C), lambda i: (0, 0)),
                  vec, vec, vec],
        out_specs=pl.BlockSpec((ts, C), lambda i: (i, 0)),
        out_shape=jax.ShapeDtypeStruct((S, C), BF16),
        scratch_shapes=[pltpu.VMEM((ts + 2 * HALO_ROWS, C), F32),
                        pltpu.VMEM((SUBLANES - 1, ts + 2 * HALO_ROWS, lc), F32),
                        pltpu.VMEM((ts, C), F32)],
        compiler_params=_params("parallel"),
        name="conv",
    )(hglu, hglu, hglu, w, b, g, beta)


def _merge_kernel(hn_ref, hc_ref, o_ref, wgc_ref, wgm_ref, wpw_ref, wo_ref, out_ref):
    hn = hn_ref[...]
    gc = jax.nn.sigmoid(jnp.dot(hn, wgc_ref[...], preferred_element_type=F32))
    gm = jax.nn.sigmoid(jnp.dot(hn, wgm_ref[...], preferred_element_type=F32))
    yc = jnp.dot(hc_ref[...], wpw_ref[...], preferred_element_type=F32)
    ym = jnp.dot(o_ref[...], wo_ref[...], preferred_element_type=F32)
    out_ref[...] = (gc * yc + gm * ym).astype(BF16)


def _merge(hn, hc, o, wgc, wgm, wpw, wo, *, tm=1024, tn=512):
    S, D = hn.shape
    tm = min(tm, S)
    row = pl.BlockSpec((tm, D), lambda i, j: (i, 0))
    col = pl.BlockSpec((D, tn), lambda i, j: (0, j))
    return pl.pallas_call(
        _merge_kernel,
        grid=(S // tm, D // tn),
        in_specs=[row, row, row, col, col, col, col],
        out_specs=pl.BlockSpec((tm, tn), lambda i, j: (i, j)),
        out_shape=jax.ShapeDtypeStruct((S, D), BF16),
        compiler_params=_params("parallel", "arbitrary"),
        name="merge",
    )(hn, hc, o, wgc, wgm, wpw, wo)


def _outproj_kernel(x_ref, m_ref, w_ref, o_ref):
    o_ref[...] = x_ref[...] + jnp.dot(m_ref[...], w_ref[...], preferred_element_type=F32)


def _out_proj(x, merged, w, *, tm=512):
    S, D = x.shape
    tm = min(tm, S)
    row = pl.BlockSpec((tm, D), lambda i: (i, 0))
    return pl.pallas_call(
        _outproj_kernel,
        grid=(S // tm,),
        in_specs=[row, row, pl.BlockSpec((D, D), lambda i: (0, 0))],
        out_specs=row,
        out_shape=jax.ShapeDtypeStruct((S, D), F32),
        compiler_params=_params("parallel"),
        name="out_proj",
    )(x, merged, w)


def _rot_half_cols(w):
    half = QK_ROPE_DIM // 2
    return jnp.concatenate([-w[..., half:], w[..., :half]], axis=-1)


def _pad_cols(w, n):
    return jnp.pad(w, [(0, 0)] * (w.ndim - 1) + [(0, n - w.shape[-1])])


def _layer(x, pos, freq, ffn1_norm_g, ffn1_w_gate, ffn1_w_up, ffn1_w_down, mix_norm_g, w_in,
           conv_w_dw, conv_b_dw, conv_ln_g, conv_ln_b, conv_w_pw_out, mla_q_norm_g, mla_w_uq,
           mla_kv_norm_g, mla_w_ukv, mla_w_o, w_out, ffn2_norm_g, ffn2_w_gate, ffn2_w_up,
           ffn2_w_down, out_norm_g, *, final):
    D = x.shape[1]
    C = conv_w_dw.shape[1]
    row = lambda v: v.reshape(1, -1)
    bf = lambda w: w.astype(BF16)

    o_q = 2 * C
    o_kv = o_q + Q_LORA_RANK
    o_kr = o_kv + KV_LORA_RANK
    o_gc = o_kr + QK_ROPE_DIM
    o_gm = o_gc + D
    w_in = lax.optimization_barrier(bf(w_in))
    w_glu = w_in[:, :o_q]
    w_kr = w_in[:, o_kr:o_gc]
    w_small = jnp.concatenate(
        [w_in[:, o_q:o_kr], _pad_cols(w_kr, LANES), _pad_cols(_rot_half_cols(w_kr), LANES)], axis=1)
    w_gc = w_in[:, o_gc:o_gm]
    w_gm = w_in[:, o_gm:]

    wq = lax.optimization_barrier(bf(mla_w_uq)).reshape(Q_LORA_RANK, N_HEADS, QK_NOPE_DIM + QK_ROPE_DIM)
    wq_rope = wq[..., QK_NOPE_DIM:]
    wq = jnp.concatenate([wq[..., :QK_NOPE_DIM], wq_rope, _rot_half_cols(wq_rope)], axis=-1)
    wqt = wq.transpose(1, 2, 0).reshape(N_HEADS * QK_PAD_DIM, Q_LORA_RANK)
    wkv = lax.optimization_barrier(bf(mla_w_ukv)).reshape(KV_LORA_RANK, N_HEADS, QK_NOPE_DIM + V_HEAD_DIM)
    wk = wkv[..., :QK_NOPE_DIM].reshape(KV_LORA_RANK, N_HEADS * QK_NOPE_DIM)
    wvt = wkv[..., QK_NOPE_DIM:].transpose(1, 2, 0).reshape(N_HEADS * V_HEAD_DIM, KV_LORA_RANK)

    x1, hn = _ffn(x, row(ffn1_norm_g), bf(ffn1_w_gate), bf(ffn1_w_up), bf(ffn1_w_down),
                  row(mix_norm_g), final=False)

    hglu = _glu_proj(hn, w_glu)
    hc = _conv(hglu, conv_w_dw, row(conv_b_dw), row(conv_ln_g), row(conv_ln_b))

    cq, ckv, kr, cst, snt = _small_proj(hn, w_small, row(mla_q_norm_g), row(mla_kv_norm_g), pos, freq)
    q_scale = (QK_NOPE_DIM + QK_ROPE_DIM) ** -0.5 * math.log2(math.e)
    qt, k, vt, qn2, kmax2 = _qkv_proj(cq, ckv, kr, cst, snt, wqt, wk, wvt, q_scale=q_scale)
    o_bounded, denom = _attention_bounded(qt, qn2, kmax2, k, vt)
    o = lax.cond(jnp.all(denom >= L_MIN), lambda: o_bounded, lambda: _attention(qt, k, vt))

    merged = _merge(hn, hc, o, w_gc, w_gm, bf(conv_w_pw_out), bf(mla_w_o))
    x2 = _out_proj(x1, merged, bf(w_out))
    (y,) = _ffn(x2, row(ffn2_norm_g), bf(ffn2_w_gate), bf(ffn2_w_up), bf(ffn2_w_down),
                row(out_norm_g), final=final)
    return y


def kernel(x, positions, ffn1_norm_g, ffn1_w_gate, ffn1_w_up, ffn1_w_down, mix_norm_g, w_in, conv_w_dw, conv_b_dw, conv_ln_g, conv_ln_b, conv_w_pw_out, mla_q_norm_g, mla_w_uq, mla_kv_norm_g, mla_w_ukv, mla_w_o, w_out, ffn2_norm_g, ffn2_w_gate, ffn2_w_up, ffn2_w_down, final_norm_g):
    B, S, D = x.shape
    depth = ffn1_norm_g.shape[0]
    assert depth == 1, "the fused final norm assumes a single layer"
    inv_freq = ROPE_THETA ** (-jnp.arange(0, QK_ROPE_DIM, 2, dtype=F32) / QK_ROPE_DIM)
    freq = inv_freq.reshape(QK_ROPE_DIM // 2, 1)
    outs = []
    for b in range(B):
        y = _layer(x[b], positions[b].reshape(1, S), freq,
                   ffn1_norm_g[0], ffn1_w_gate[0], ffn1_w_up[0], ffn1_w_down[0], mix_norm_g[0],
                   w_in[0], conv_w_dw[0], conv_b_dw[0], conv_ln_g[0], conv_ln_b[0],
                   conv_w_pw_out[0], mla_q_norm_g[0], mla_w_uq[0], mla_kv_norm_g[0], mla_w_ukv[0],
                   mla_w_o[0], w_out[0], ffn2_norm_g[0], ffn2_w_gate[0], ffn2_w_up[0],
                   ffn2_w_down[0], final_norm_g, final=True)
        outs.append(y)
    return jnp.stack(outs)
```

```python
import functools
import math

import jax
import jax.numpy as jnp
from jax import lax
from jax.experimental import pallas as pl
from jax.experimental.pallas import tpu as pltpu

N_HEADS = 16
QK_NOPE_DIM = 128
QK_ROPE_DIM = 64
V_HEAD_DIM = 128
Q_LORA_RANK = 768
KV_LORA_RANK = 512
CONV_WIDTH = 31
CONV_PAD = CONV_WIDTH // 2
ROPE_THETA = 10000.0
NORM_EPS = 1e-6

LANES = 128
SUBLANES = 8
QK_PAD_DIM = 256
QK_DIM = QK_NOPE_DIM + QK_ROPE_DIM
V_EXT_DIM = V_HEAD_DIM + 16
BOUND_MARGIN = 1.0 + 2.0 ** -6
L_MIN = 2.0 ** -80
HALO_ROWS = 16
VMEM_LIMIT = 56 * 1024 * 1024

F32 = jnp.float32
BF16 = jnp.bfloat16


def _params(*sem):
    return pltpu.CompilerParams(dimension_semantics=sem, vmem_limit_bytes=VMEM_LIMIT)


def _rms(x, g):
    return x * lax.rsqrt(jnp.mean(x * x, axis=-1, keepdims=True) + NORM_EPS) * g


def _ffn_kernel(x_ref, g_ref, wg_ref, wu_ref, wd_ref, g2_ref, *refs, final):
    if final:
        o_ref, hn_scr, acc_scr = refs
    else:
        o_ref, hn_out_ref, hn_scr, acc_scr = refs
    j = pl.program_id(1)

    @pl.when(j == 0)
    def _():
        hn_scr[...] = _rms(x_ref[...], g_ref[...]).astype(BF16)
        acc_scr[...] = jnp.zeros_like(acc_scr)

    h = hn_scr[...]
    half = wg_ref.shape[1] // 2
    acts = []
    for c in (0, half):
        a = jnp.dot(h, wg_ref[:, c:c + half], preferred_element_type=F32)
        u = jnp.dot(h, wu_ref[:, c:c + half], preferred_element_type=F32)
        acts.append((a * jax.nn.sigmoid(a) * u).astype(BF16))
    act = jnp.concatenate(acts, axis=1)
    acc_scr[...] += jnp.dot(act, wd_ref[...], preferred_element_type=F32)

    @pl.when(j == pl.num_programs(1) - 1)
    def _():
        y = x_ref[...] + 0.5 * acc_scr[...]
        if final:
            o_ref[...] = _rms(y, g2_ref[...])
        else:
            o_ref[...] = y
            hn_out_ref[...] = _rms(y, g2_ref[...]).astype(BF16)


def _ffn(x, g, wg, wu, wd, g2, *, final, tm=512, tf=512):
    S, D = x.shape
    F = wg.shape[1]
    tm = min(tm, S)
    row = pl.BlockSpec((tm, D), lambda i, j: (i, 0))
    vec = pl.BlockSpec((1, D), lambda i, j: (0, 0))
    out_shape = [jax.ShapeDtypeStruct((S, D), F32)]
    out_specs = [row]
    if not final:
        out_shape.append(jax.ShapeDtypeStruct((S, D), BF16))
        out_specs.append(row)
    return pl.pallas_call(
        functools.partial(_ffn_kernel, final=final),
        grid=(S // tm, F // tf),
        in_specs=[row, vec,
                  pl.BlockSpec((D, tf), lambda i, j: (0, j)),
                  pl.BlockSpec((D, tf), lambda i, j: (0, j)),
                  pl.BlockSpec((tf, D), lambda i, j: (j, 0)),
                  vec],
        out_specs=out_specs,
        out_shape=out_shape,
        scratch_shapes=[pltpu.VMEM((tm, D), BF16), pltpu.VMEM((tm, D), F32)],
        compiler_params=_params("parallel", "arbitrary"),
        name="ffn_final" if final else "ffn_mix",
    )(x, g, wg, wu, wd, g2)


def _glu_kernel(h_ref, wa_ref, wg_ref, o_ref):
    h = h_ref[...]
    for c in range(0, wa_ref.shape[1], QK_PAD_DIM):
        a = jnp.dot(h, wa_ref[:, c:c + QK_PAD_DIM], preferred_element_type=F32)
        g = jnp.dot(h, wg_ref[:, c:c + QK_PAD_DIM], preferred_element_type=F32)
        o_ref[:, c:c + QK_PAD_DIM] = a * jax.nn.sigmoid(g)


def _glu_proj(hn, w_glu, *, tm=1024, tn=1024):
    S, D = hn.shape
    C = w_glu.shape[1] // 2
    tm = min(tm, S)
    nj = C // tn
    return pl.pallas_call(
        _glu_kernel,
        grid=(S // tm, nj),
        in_specs=[pl.BlockSpec((tm, D), lambda i, j: (i, 0)),
                  pl.BlockSpec((D, tn), lambda i, j: (0, j)),
                  pl.BlockSpec((D, tn), lambda i, j: (0, j + nj))],
        out_specs=pl.BlockSpec((tm, tn), lambda i, j: (i, j)),
        out_shape=jax.ShapeDtypeStruct((S, C), F32),
        compiler_params=_params("parallel", "arbitrary"),
        name="glu_proj",
    )(hn, w_glu, w_glu)


def _small_kernel(h_ref, w_ref, gq_ref, gkv_ref, pos_ref, freq_ref,
                  cq_ref, ckv_ref, kr_ref, cst_ref, snt_ref):
    r = jnp.dot(h_ref[...], w_ref[...], preferred_element_type=F32)
    cq_ref[...] = _rms(r[:, :Q_LORA_RANK], gq_ref[...]).astype(BF16)
    c0 = Q_LORA_RANK
    ckv_ref[...] = _rms(r[:, c0:c0 + KV_LORA_RANK], gkv_ref[...]).astype(BF16)
    c1 = c0 + KV_LORA_RANK
    ang = freq_ref[...] * pos_ref[...].astype(F32)
    cos, sin = jnp.cos(ang), jnp.sin(ang)
    cst = jnp.concatenate([cos, cos], axis=0)
    snt = jnp.concatenate([sin, sin], axis=0)
    cst_ref[...] = cst
    snt_ref[...] = snt
    pad = LANES - QK_ROPE_DIM
    cs = jnp.concatenate([cst, jnp.ones((pad, cst.shape[1]), F32)], axis=0).T
    sn = jnp.concatenate([snt, jnp.zeros((pad, snt.shape[1]), F32)], axis=0).T
    kr = r[:, c1:c1 + LANES] * cs + r[:, c1 + LANES:c1 + 2 * LANES] * sn
    lane = lax.broadcasted_iota(jnp.int32, kr.shape, 1)
    kr_ref[...] = jnp.where(lane == QK_ROPE_DIM, 1.0, kr).astype(BF16)


def _small_proj(hn, w_small, gq, gkv, pos, freq, *, tm=512):
    S, D = hn.shape
    tm = min(tm, S)
    row = lambda n: pl.BlockSpec((tm, n), lambda i: (i, 0))
    col = lambda n: pl.BlockSpec((n, tm), lambda i: (0, i))
    full = lambda a: pl.BlockSpec(a.shape, lambda i: (0, 0))
    return pl.pallas_call(
        _small_kernel,
        grid=(S // tm,),
        in_specs=[row(D), full(w_small), full(gq), full(gkv), col(1), full(freq)],
        out_specs=[row(Q_LORA_RANK), row(KV_LORA_RANK), row(LANES), col(QK_ROPE_DIM), col(QK_ROPE_DIM)],
        out_shape=[jax.ShapeDtypeStruct((S, Q_LORA_RANK), BF16),
                   jax.ShapeDtypeStruct((S, KV_LORA_RANK), BF16),
                   jax.ShapeDtypeStruct((S, LANES), BF16),
                   jax.ShapeDtypeStruct((QK_ROPE_DIM, S), F32),
                   jax.ShapeDtypeStruct((QK_ROPE_DIM, S), F32)],
        compiler_params=_params("parallel"),
        name="small_proj",
    )(hn, w_small, gq, gkv, pos, freq)


_NT = (((1,), (1,)), ((), ()))


def _sq(x):
    x = x.astype(F32)
    return x * x


def _qkv_kernel(cq_ref, ckv_ref, kr_ref, cst_ref, snt_ref, wqt_ref, wk_ref, wvt_ref,
                qt_ref, k_ref, vt_ref, qn2_ref, kmax2_ref, *, q_scale):
    @pl.when(pl.program_id(0) == 0)
    def _():
        kmax2_ref[...] = jnp.zeros_like(kmax2_ref)

    cq = cq_ref[...]
    ckv = ckv_ref[...]
    kr = kr_ref[...]
    cst = cst_ref[...]
    snt = snt_ref[...]
    kr_n2 = jnp.sum(_sq(kr), axis=1, keepdims=True) - 1.0
    rt_all = lax.dot_general(wqt_ref[...], cq, _NT, preferred_element_type=F32)
    k_all = jnp.dot(ckv, wk_ref[...], preferred_element_type=F32)
    vt_all = lax.dot_general(wvt_ref[...], ckv, _NT, preferred_element_type=F32)
    for h in range(N_HEADS):
        rt = rt_all[h * QK_PAD_DIM:(h + 1) * QK_PAD_DIM]
        n0, n1, n2 = QK_NOPE_DIM, QK_DIM, QK_DIM + QK_ROPE_DIM
        q_nope = (rt[0:n0] * q_scale).astype(BF16)
        q_rope = ((rt[n0:n1] * cst + rt[n1:n2] * snt) * q_scale).astype(BF16)
        qt_ref[h, 0:n0, :] = q_nope
        qt_ref[h, n0:n1, :] = q_rope
        qt_ref[h, n1:QK_PAD_DIM, :] = jnp.zeros((QK_PAD_DIM - n1, rt.shape[1]), BF16)
        k_nope = k_all[:, h * QK_NOPE_DIM:(h + 1) * QK_NOPE_DIM].astype(BF16)
        k_ref[h, :, 0:LANES] = k_nope
        k_ref[h, :, LANES:2 * LANES] = kr
        vt = vt_all[h * V_HEAD_DIM:(h + 1) * V_HEAD_DIM]
        vt_ref[h, 0:V_HEAD_DIM, :] = vt.astype(BF16)
        vt_ref[h, V_HEAD_DIM:V_EXT_DIM, :] = jnp.ones((V_EXT_DIM - V_HEAD_DIM, vt.shape[1]), BF16)
        qn2_ref[h] = jnp.sum(_sq(q_nope), axis=0, keepdims=True) + jnp.sum(_sq(q_rope), axis=0, keepdims=True)
        k_n2 = jnp.sum(_sq(k_nope), axis=1, keepdims=True) + kr_n2
        kmax2_ref[h] = jnp.maximum(kmax2_ref[h], jnp.max(k_n2, axis=0, keepdims=True))


def _qkv_proj(cq, ckv, kr, cst, snt, wqt, wk, wvt, *, q_scale, tm=256):
    S = cq.shape[0]
    tm = min(tm, S)
    row = lambda n: pl.BlockSpec((tm, n), lambda i: (i, 0))
    col = lambda n: pl.BlockSpec((n, tm), lambda i: (0, i))
    full2 = lambda a: pl.BlockSpec(a.shape, lambda i: (0, 0))
    return pl.pallas_call(
        functools.partial(_qkv_kernel, q_scale=q_scale),
        grid=(S // tm,),
        in_specs=[row(Q_LORA_RANK), row(KV_LORA_RANK), row(LANES), col(QK_ROPE_DIM), col(QK_ROPE_DIM),
                  full2(wqt), full2(wk), full2(wvt)],
        out_specs=[pl.BlockSpec((N_HEADS, QK_PAD_DIM, tm), lambda i: (0, 0, i)),
                   pl.BlockSpec((N_HEADS, tm, QK_PAD_DIM), lambda i: (0, i, 0)),
                   pl.BlockSpec((N_HEADS, V_EXT_DIM, tm), lambda i: (0, 0, i)),
                   pl.BlockSpec((N_HEADS, 1, tm), lambda i: (0, 0, i)),
                   pl.BlockSpec((N_HEADS, SUBLANES, LANES), lambda i: (0, 0, 0))],
        out_shape=[jax.ShapeDtypeStruct((N_HEADS, QK_PAD_DIM, S), BF16),
                   jax.ShapeDtypeStruct((N_HEADS, S, QK_PAD_DIM), BF16),
                   jax.ShapeDtypeStruct((N_HEADS, V_EXT_DIM, S), BF16),
                   jax.ShapeDtypeStruct((N_HEADS, 1, S), F32),
                   jax.ShapeDtypeStruct((N_HEADS, SUBLANES, LANES), F32)],
        compiler_params=_params("arbitrary"),
        name="qkv_proj",
    )(cq, ckv, kr, cst, snt, wqt, wk, wvt)


def _attn_bounded_kernel(qt_ref, qn2_ref, kmax2_ref, k_ref, vt_ref, o_ref, l_ref, *, tk, unroll):
    tq = qt_ref.shape[1]
    n_kv = k_ref.shape[0] // tk
    bound = jnp.sqrt(qn2_ref[...] * kmax2_ref[0:1, 0:1]) * BOUND_MARGIN
    tile = 2 * SUBLANES
    row = lax.broadcasted_iota(jnp.int32, (tile, tq), 0)
    offset_rows = jnp.where(row == 0, -bound, 0.0).astype(BF16)
    qt = jnp.concatenate([qt_ref[0:QK_DIM, :], offset_rows, qt_ref[QK_DIM + tile:, :]], axis=0)

    def body(u, carry):
        acc, l = carry
        for j in range(unroll):
            start = pl.multiple_of((unroll * u + j) * tk, tk)
            s = jnp.dot(k_ref[pl.ds(start, tk), :], qt, preferred_element_type=F32)
            p = jnp.exp2(s)
            l = l + jnp.sum(p, axis=0, keepdims=True)
            acc = acc + jnp.dot(vt_ref[:, pl.ds(start, tk)], p.astype(BF16), preferred_element_type=F32)
        return acc, l

    zeros = (jnp.zeros((V_HEAD_DIM, tq), F32), jnp.zeros((1, tq), F32))
    acc, l = lax.fori_loop(0, n_kv // unroll, body, zeros)
    l_ref[...] = l
    o_ref[...] = (acc / l).T.astype(BF16)


def _attention_bounded(qt, qn2, kmax2, k, vt, *, tq=1024, tk=2048, unroll=8):
    H, S, _ = k.shape
    tq = min(tq, S)
    tk = min(tk, S // unroll)
    assert S % (unroll * tk) == 0 and S % tq == 0
    return pl.pallas_call(
        functools.partial(_attn_bounded_kernel, tk=tk, unroll=unroll),
        grid=(H, S // tq),
        in_specs=[pl.BlockSpec((None, QK_PAD_DIM, tq), lambda h, i: (h, 0, i)),
                  pl.BlockSpec((None, 1, tq), lambda h, i: (h, 0, i)),
                  pl.BlockSpec((None, SUBLANES, LANES), lambda h, i: (h, 0, 0)),
                  pl.BlockSpec((None, S, QK_PAD_DIM), lambda h, i: (h, 0, 0)),
                  pl.BlockSpec((None, V_HEAD_DIM, S), lambda h, i: (h, 0, 0))],
        out_specs=[pl.BlockSpec((tq, V_HEAD_DIM), lambda h, i: (i, h)),
                   pl.BlockSpec((None, 1, tq), lambda h, i: (h, 0, i))],
        out_shape=[jax.ShapeDtypeStruct((S, H * V_HEAD_DIM), BF16),
                   jax.ShapeDtypeStruct((H, 1, S), F32)],
        compiler_params=_params("parallel", "arbitrary"),
        name="attention_bounded",
    )(qt, qn2, kmax2, k, vt)


def _attn_kernel(qt_ref, k_ref, vt_ref, o_ref, s_scr, *, tk, unroll):
    qt = qt_ref[...]
    tq = qt.shape[1]
    n_kv = k_ref.shape[0] // tk

    def scores(t, slot):
        start = pl.multiple_of(t * tk, tk)
        s = jnp.dot(k_ref[pl.ds(start, tk), :], qt, preferred_element_type=F32)
        s_scr[slot] = s
        return jnp.max(s, axis=0, keepdims=True)

    def softmax_pv(slot, s_max, t, m, acc):
        m_new = jnp.maximum(m, s_max)
        alpha = jnp.exp2(m - m_new)
        p = jnp.exp2(s_scr[slot] - m_new)
        start = pl.multiple_of(t * tk, tk)
        pv = jnp.dot(vt_ref[:, pl.ds(start, tk)], p.astype(BF16), preferred_element_type=F32)
        return m_new, alpha * acc + pv

    def body(u, carry):
        m, acc, s_max = carry
        t = unroll * u
        for j in range(unroll):
            nxt = t + j + 1
            if j == unroll - 1:
                nxt = jnp.minimum(nxt, n_kv - 1)
            next_max = scores(nxt, (j + 1) % 2)
            m, acc = softmax_pv(j % 2, s_max, t + j, m, acc)
            s_max = next_max
        return m, acc, s_max

    m0 = jnp.full((1, tq), -jnp.inf, F32)
    acc0 = jnp.zeros((V_EXT_DIM, tq), F32)
    _, acc, _ = lax.fori_loop(0, n_kv // unroll, body, (m0, acc0, scores(0, 0)))
    o_ref[...] = (acc[:V_HEAD_DIM] / acc[V_HEAD_DIM:V_HEAD_DIM + 1]).T.astype(BF16)


def _attention(qt, k, vt, *, tq=512, tk=512, unroll=16):
    H, S, _ = k.shape
    tq = min(tq, S)
    tk = min(tk, S // unroll)
    assert unroll % 2 == 0 and S % (unroll * tk) == 0 and S % tq == 0
    return pl.pallas_call(
        functools.partial(_attn_kernel, tk=tk, unroll=unroll),
        grid=(H, S // tq),
        in_specs=[pl.BlockSpec((None, QK_PAD_DIM, tq), lambda h, i: (h, 0, i)),
                  pl.BlockSpec((None, S, QK_PAD_DIM), lambda h, i: (h, 0, 0)),
                  pl.BlockSpec((None, V_EXT_DIM, S), lambda h, i: (h, 0, 0))],
        out_specs=pl.BlockSpec((tq, V_HEAD_DIM), lambda h, i: (i, h)),
        out_shape=jax.ShapeDtypeStruct((S, H * V_HEAD_DIM), BF16),
        scratch_shapes=[pltpu.VMEM((2, tk, tq), F32)],
        compiler_params=_params("parallel", "arbitrary"),
        name="attention",
    )(qt, k, vt)


def _conv_kernel(prev_ref, cur_ref, next_ref, w_ref, b_ref, g_ref, beta_ref, o_ref,
                 buf, shifted, conv_scr, *, rb, lc):
    i = pl.program_id(0)
    ts, C = cur_ref.shape
    first = i == 0
    last = i == pl.num_programs(0) - 1
    buf[0:HALO_ROWS, :] = jnp.where(first, 0.0, prev_ref[...])
    buf[HALO_ROWS:HALO_ROWS + ts, :] = cur_ref[...]
    buf[HALO_ROWS + ts:, :] = jnp.where(last, 0.0, next_ref[...])
    off = HALO_ROWS - CONV_PAD
    n_sh = ts + SUBLANES * ((CONV_WIDTH + off - 1) // SUBLANES)

    for c0 in range(0, C, lc):
        for b in range(1, SUBLANES):
            shifted[b - 1, 0:n_sh, :] = buf[b:b + n_sh, c0:c0 + lc]

        def rows(rblk, _, c0=c0):
            r0 = pl.multiple_of(rblk * rb, rb)
            acc = jnp.broadcast_to(b_ref[:, c0:c0 + lc], (rb, lc))
            for k in range(CONV_WIDTH):
                a, b = divmod(k + off, SUBLANES)
                start = pl.multiple_of(r0 + a * SUBLANES, SUBLANES)
                if b == 0:
                    tap = buf[pl.ds(start, rb), c0:c0 + lc]
                else:
                    tap = shifted[b - 1, pl.ds(start, rb), :]
                acc = acc + w_ref[k:k + 1, c0:c0 + lc] * tap
            conv_scr[pl.ds(r0, rb), c0:c0 + lc] = acc
            return 0
        lax.fori_loop(0, ts // rb, rows, 0)

    y = conv_scr[...]
    mu = jnp.mean(y, axis=-1, keepdims=True)
    yc = y - mu
    var = jnp.mean(yc * yc, axis=-1, keepdims=True)
    z = yc * lax.rsqrt(var + NORM_EPS) * g_ref[...] + beta_ref[...]
    o_ref[...] = (z * jax.nn.sigmoid(z)).astype(BF16)


def _conv(hglu, w, b, g, beta, *, ts=256, rb=128, lc=128):
    S, C = hglu.shape
    ts = min(ts, S)
    n = S // ts
    hb = ts // HALO_ROWS
    last_halo = S // HALO_ROWS - 1
    vec = pl.BlockSpec((1, C), lambda i: (0, 0))
    return pl.pallas_call(
        functools.partial(_conv_kernel, rb=rb, lc=lc),
        grid=(n,),
        in_specs=[pl.BlockSpec((HALO_ROWS, C), lambda i: (jnp.maximum(i * hb - 1, 0), 0)),
                  pl.BlockSpec((ts, C), lambda i: (i, 0)),
                  pl.BlockSpec((HALO_ROWS, C), lambda i: (jnp.minimum((i + 1) * hb, last_halo), 0)),
                  pl.BlockSpec((CONV_WIDTH, C), lambda i: (0, 0)),
                  vec, vec, vec],
        out_specs=pl.BlockSpec((ts, C), lambda i: (i, 0)),
        out_shape=jax.ShapeDtypeStruct((S, C), BF16),
        scratch_shapes=[pltpu.VMEM((ts + 2 * HALO_ROWS, C), F32),
                        pltpu.VMEM((SUBLANES - 1, ts + 2 * HALO_ROWS, lc), F32),
                        pltpu.VMEM((ts, C), F32)],
        compiler_params=_params("parallel"),
        name="conv",
    )(hglu, hglu, hglu, w, b, g, beta)


def _merge_kernel(hn_ref, hc_ref, o_ref, wgc_ref, wgm_ref, wpw_ref, wo_ref, out_ref):
    hn = hn_ref[...]
    gc = jax.nn.sigmoid(jnp.dot(hn, wgc_ref[...], preferred_element_type=F32))
    gm = jax.nn.sigmoid(jnp.dot(hn, wgm_ref[...], preferred_element_type=F32))
    yc = jnp.dot(hc_ref[...], wpw_ref[...], preferred_element_type=F32)
    ym = jnp.dot(o_ref[...], wo_ref[...], preferred_element_type=F32)
    out_ref[...] = (gc * yc + gm * ym).astype(BF16)


def _merge(hn, hc, o, wgc, wgm, wpw, wo, *, tm=1024, tn=512):
    S, D = hn.shape
    tm = min(tm, S)
    row = pl.BlockSpec((tm, D), lambda i, j: (i, 0))
    col = pl.BlockSpec((D, tn), lambda i, j: (0, j))
    return pl.pallas_call(
        _merge_kernel,
        grid=(S // tm, D // tn),
        in_specs=[row, row, row, col, col, col, col],
        out_specs=pl.BlockSpec((tm, tn), lambda i, j: (i, j)),
        out_shape=jax.ShapeDtypeStruct((S, D), BF16),
        compiler_params=_params("parallel", "arbitrary"),
        name="merge",
    )(hn, hc, o, wgc, wgm, wpw, wo)


def _outproj_kernel(x_ref, m_ref, w_ref, o_ref):
    o_ref[...] = x_ref[...] + jnp.dot(m_ref[...], w_ref[...], preferred_element_type=F32)


def _out_proj(x, merged, w, *, tm=512):
    S, D = x.shape
    tm = min(tm, S)
    row = pl.BlockSpec((tm, D), lambda i: (i, 0))
    return pl.pallas_call(
        _outproj_kernel,
        grid=(S // tm,),
        in_specs=[row, row, pl.BlockSpec((D, D), lambda i: (0, 0))],
        out_specs=row,
        out_shape=jax.ShapeDtypeStruct((S, D), F32),
        compiler_params=_params("parallel"),
        name="out_proj",
    )(x, merged, w)


def _rot_half_cols(w):
    half = QK_ROPE_DIM // 2
    return jnp.concatenate([-w[..., half:], w[..., :half]], axis=-1)


def _pad_cols(w, n):
    return jnp.pad(w, [(0, 0)] * (w.ndim - 1) + [(0, n - w.shape[-1])])


def _layer(x, pos, freq, ffn1_norm_g, ffn1_w_gate, ffn1_w_up, ffn1_w_down, mix_norm_g, w_in,
           conv_w_dw, conv_b_dw, conv_ln_g, conv_ln_b, conv_w_pw_out, mla_q_norm_g, mla_w_uq,
           mla_kv_norm_g, mla_w_ukv, mla_w_o, w_out, ffn2_norm_g, ffn2_w_gate, ffn2_w_up,
           ffn2_w_down, out_norm_g, *, final):
    D = x.shape[1]
    C = conv_w_dw.shape[1]
    row = lambda v: v.reshape(1, -1)
    bf = lambda w: w.astype(BF16)

    o_q = 2 * C
    o_kv = o_q + Q_LORA_RANK
    o_kr = o_kv + KV_LORA_RANK
    o_gc = o_kr + QK_ROPE_DIM
    o_gm = o_gc + D
    w_in = lax.optimization_barrier(bf(w_in))
    w_glu = w_in[:, :o_q]
    w_kr = w_in[:, o_kr:o_gc]
    w_small = jnp.concatenate(
        [w_in[:, o_q:o_kr], _pad_cols(w_kr, LANES), _pad_cols(_rot_half_cols(w_kr), LANES)], axis=1)
    w_gc = w_in[:, o_gc:o_gm]
    w_gm = w_in[:, o_gm:]

    wq = lax.optimization_barrier(bf(mla_w_uq)).reshape(Q_LORA_RANK, N_HEADS, QK_NOPE_DIM + QK_ROPE_DIM)
    wq_rope = wq[..., QK_NOPE_DIM:]
    wq = jnp.concatenate([wq[..., :QK_NOPE_DIM], wq_rope, _rot_half_cols(wq_rope)], axis=-1)
    wqt = wq.transpose(1, 2, 0).reshape(N_HEADS * QK_PAD_DIM, Q_LORA_RANK)
    wkv = lax.optimization_barrier(bf(mla_w_ukv)).reshape(KV_LORA_RANK, N_HEADS, QK_NOPE_DIM + V_HEAD_DIM)
    wk = wkv[..., :QK_NOPE_DIM].reshape(KV_LORA_RANK, N_HEADS * QK_NOPE_DIM)
    wvt = wkv[..., QK_NOPE_DIM:].transpose(1, 2, 0).reshape(N_HEADS * V_HEAD_DIM, KV_LORA_RANK)

    x1, hn = _ffn(x, row(ffn1_norm_g), bf(ffn1_w_gate), bf(ffn1_w_up), bf(ffn1_w_down),
                  row(mix_norm_g), final=False)

    hglu = _glu_proj(hn, w_glu)
    hc = _conv(hglu, conv_w_dw, row(conv_b_dw), row(conv_ln_g), row(conv_ln_b))

    cq, ckv, kr, cst, snt = _small_proj(hn, w_small, row(mla_q_norm_g), row(mla_kv_norm_g), pos, freq)
    q_scale = (QK_NOPE_DIM + QK_ROPE_DIM) ** -0.5 * math.log2(math.e)
    qt, k, vt, qn2, kmax2 = _qkv_proj(cq, ckv, kr, cst, snt, wqt, wk, wvt, q_scale=q_scale)
    o_bounded, denom = _attention_bounded(qt, qn2, kmax2, k, vt)
    o = lax.cond(jnp.all(denom >= L_MIN), lambda: o_bounded, lambda: _attention(qt, k, vt))

    merged = _merge(hn, hc, o, w_gc, w_gm, bf(conv_w_pw_out), bf(mla_w_o))
    x2 = _out_proj(x1, merged, bf(w_out))
    (y,) = _ffn(x2, row(ffn2_norm_g), bf(ffn2_w_gate), bf(ffn2_w_up), bf(ffn2_w_down),
                row(out_norm_g), final=final)
    return y


def kernel(x, positions, ffn1_norm_g, ffn1_w_gate, ffn1_w_up, ffn1_w_down, mix_norm_g, w_in, conv_w_dw, conv_b_dw, conv_ln_g, conv_ln_b, conv_w_pw_out, mla_q_norm_g, mla_w_uq, mla_kv_norm_g, mla_w_ukv, mla_w_o, w_out, ffn2_norm_g, ffn2_w_gate, ffn2_w_up, ffn2_w_down, final_norm_g):
    B, S, D = x.shape
    depth = ffn1_norm_g.shape[0]
    assert depth == 1, "the fused final norm assumes a single layer"
    inv_freq = ROPE_THETA ** (-jnp.arange(0, QK_ROPE_DIM, 2, dtype=F32) / QK_ROPE_DIM)
    freq = inv_freq.reshape(QK_ROPE_DIM // 2, 1)
    outs = []
    for b in range(B):
        y = _layer(x[b], positions[b].reshape(1, S), freq,
                   ffn1_norm_g[0], ffn1_w_gate[0], ffn1_w_up[0], ffn1_w_down[0], mix_norm_g[0],
                   w_in[0], conv_w_dw[0], conv_b_dw[0], conv_ln_g[0], conv_ln_b[0],
                   conv_w_pw_out[0], mla_q_norm_g[0], mla_w_uq[0], mla_kv_norm_g[0], mla_w_ukv[0],
                   mla_w_o[0], w_out[0], ffn2_norm_g[0], ffn2_w_gate[0], ffn2_w_up[0],
                   ffn2_w_down[0], final_norm_g, final=True)
        outs.append(y)
    return jnp.stack(outs)
```
